```python
import math
import jax, jax.numpy as jnp
from jax import lax
import numpy as np

D_MODEL = 1024
BATCH = 4
SEQ = 4096
DEPTH = 2
DEC_BATCH = 128
DEC_SEQ = 8
PAST_LEN = 2048
PAGE_SIZE = 128

N_HEADS = 8
HEAD_DIM = 64
ATT_WIDTH = N_HEADS * HEAD_DIM
D_SSM = D_MODEL // 2
GROUP_CH = 16
N_GROUPS = D_SSM // GROUP_CH
SSM_STATE = 64
D_FF = 4 * D_MODEL
PLE_DIM = 256
Q_BLOCK = 128
RMS_EPS = 1e-6
DT_MIN = 1e-3
DT_MAX = 1e-1
SB_BIAS_INIT = -6.0
IN_WIDTH = 3 * ATT_WIDTH + D_SSM + 2 * D_MODEL

kernel_name = "hybrid_s5_stickbreaking_decoder_step"


def rms_norm(x, g):
    xf = x.astype(jnp.float32)
    y = xf * lax.rsqrt(jnp.mean(xf * xf, axis=-1, keepdims=True) + RMS_EPS)
    return (y * g.astype(jnp.float32)).astype(x.dtype)


def sb_attend(q, k, v, bias, q_pos):
    scale = HEAD_DIM ** -0.5
    z = (jnp.einsum('bqhd,bkhd->bhqk', q.astype(jnp.float32), k.astype(jnp.float32)) * scale
         + bias.astype(jnp.float32)[None, :, None, None])
    k_pos = jnp.arange(k.shape[1], dtype=jnp.int32)
    mask = k_pos[None, :] < q_pos[:, None]
    log_stay = jnp.where(mask, jax.nn.log_sigmoid(-z), 0.0)
    suffix = lax.cumsum(log_stay, axis=3, reverse=True) - log_stay
    w = jnp.where(mask, jnp.exp(jax.nn.log_sigmoid(z) + suffix), 0.0)
    return jnp.einsum('bhqk,bkhd->bqhd', w, v.astype(jnp.float32)).astype(v.dtype)


def sb_prompt(q, k, v, bias):
    b, s = q.shape[0], q.shape[1]
    nblk = s // Q_BLOCK
    q_blocks = jnp.moveaxis(q.reshape(b, nblk, Q_BLOCK, N_HEADS, HEAD_DIM), 1, 0)
    pos_blocks = jnp.arange(s, dtype=jnp.int32).reshape(nblk, Q_BLOCK)
    o = lax.map(lambda qp: sb_attend(qp[0], k, v, bias, qp[1]), (q_blocks, pos_blocks))
    return jnp.moveaxis(o, 0, 1).reshape(b, s, N_HEADS, HEAD_DIM)


def make_sample_attend(past_k, past_v):
    def attend(q, k, v, bias):
        k_all = jnp.concatenate([past_k, k], axis=1)
        v_all = jnp.concatenate([past_v, v], axis=1)
        q_pos = past_k.shape[1] + jnp.arange(q.shape[1], dtype=jnp.int32)
        return sb_attend(q, k_all, v_all, bias, q_pos)
    return attend


def s5_branch(u, h0, a_re, a_im, log_dt, b_re, b_im, c_re, c_im, d_skip):
    bsz, t = u.shape[0], u.shape[1]
    f32 = jnp.float32
    uf = u.astype(f32)
    ug = uf.reshape(bsz, t, N_GROUPS, GROUP_CH)
    a_re, a_im = a_re.astype(f32), a_im.astype(f32)
    dt = jnp.exp(log_dt.astype(f32))[:, None]
    mag = jnp.exp(a_re * dt)
    ab_re, ab_im = mag * jnp.cos(a_im * dt), mag * jnp.sin(a_im * dt)
    den = a_re * a_re + a_im * a_im
    nr, ni = ab_re - 1.0, ab_im
    f_re = (nr * a_re + ni * a_im) / den
    f_im = (ni * a_re - nr * a_im) / den
    b_re, b_im = b_re.astype(f32), b_im.astype(f32)
    bb_re = f_re[..., None] * b_re - f_im[..., None] * b_im
    bb_im = f_re[..., None] * b_im + f_im[..., None] * b_re
    bu_re = jnp.einsum('btgc,gpc->btgp', ug, bb_re)
    bu_im = jnp.einsum('btgc,gpc->btgp', ug, bb_im)
    h0_re, h0_im = h0[..., 0].astype(f32), h0[..., 1].astype(f32)
    bu_re = bu_re.at[:, 0].add(ab_re * h0_re - ab_im * h0_im)
    bu_im = bu_im.at[:, 0].add(ab_re * h0_im + ab_im * h0_re)
    ar_full = jnp.broadcast_to(ab_re, bu_re.shape)
    ai_full = jnp.broadcast_to(ab_im, bu_im.shape)

    def combine(e1, e2):
        a1r, a1i, b1r, b1i = e1
        a2r, a2i, b2r, b2i = e2
        return (a2r * a1r - a2i * a1i, a2r * a1i + a2i * a1r,
                a2r * b1r - a2i * b1i + b2r, a2r * b1i + a2i * b1r + b2i)

    _, _, hr, hi = lax.associative_scan(combine, (ar_full, ai_full, bu_re, bu_im), axis=1)
    y = (jnp.einsum('btgp,gcp->btgc', hr, c_re.astype(f32))
         - jnp.einsum('btgp,gcp->btgc', hi, c_im.astype(f32)))
    y = y.reshape(bsz, t, D_SSM) + d_skip.astype(f32) * uf
    h_last = jnp.stack([hr[:, -1], hi[:, -1]], axis=-1).astype(h0.dtype)
    return y, h_last


def decoder_layer(x, p_l, h0, attend, g_mix, w_in, g_q, g_k, sb_bias, w_attn_out,
                  a_re, a_im, log_dt, b_re, b_im, c_re, c_im, d_skip,
                  w_glu_val, w_glu_gate, w_out, g_mlp, w_up, w_down, g_ple, w_ple, w_ple_gate):
    b, t = x.shape[0], x.shape[1]
    h = rms_norm(x, g_mix)
    proj = h @ w_in
    q, k, v, u, gate_a, gate_s = jnp.split(
        proj, [ATT_WIDTH, 2 * ATT_WIDTH, 3 * ATT_WIDTH, 3 * ATT_WIDTH + D_SSM,
               3 * ATT_WIDTH + D_SSM + D_MODEL], axis=-1)
    q = rms_norm(q.reshape(b, t, N_HEADS, HEAD_DIM), g_q)
    k = rms_norm(k.reshape(b, t, N_HEADS, HEAD_DIM), g_k)
    v = v.reshape(b, t, N_HEADS, HEAD_DIM)
    o = attend(q, k, v, sb_bias)
    attn_out = o.reshape(b, t, ATT_WIDTH) @ w_attn_out
    y_ssm, h_last = s5_branch(u, h0, a_re, a_im, log_dt, b_re, b_im, c_re, c_im, d_skip)
    y_act = jax.nn.gelu(y_ssm).astype(x.dtype)
    ssm_out = (y_act @ w_glu_val) * jax.nn.sigmoid(y_act @ w_glu_gate)
    merged = jax.nn.sigmoid(gate_a) * attn_out + jax.nn.sigmoid(gate_s) * ssm_out
    x = x + merged @ w_out
    h2 = rms_norm(x, g_mlp)
    x = x + jnp.square(jax.nn.relu(h2 @ w_up)) @ w_down
    h3 = rms_norm(x, g_ple)
    x = x + (p_l @ w_ple) * jax.nn.sigmoid(h3 @ w_ple_gate)
    return x, k, v, h_last


def setup_inputs(seed: int = 0) -> dict:
    key = jax.random.key(seed)
    ks = jax.random.split(key, 40)
    f32 = jnp.float32
    n_pages = PAST_LEN // PAGE_SIZE
    n_used = DEC_BATCH * n_pages
    n_phys = (n_used * 5) // 4

    def nrm(k, shape, scale):
        return jax.random.normal(k, shape, f32) * scale

    page_table = jax.random.permutation(ks[0], n_phys)[:n_used].reshape(DEC_BATCH, n_pages).astype(jnp.int32)
    n_idx = jnp.arange(SSM_STATE, dtype=f32)
    return {
        "x_prompt": nrm(ks[1], (BATCH, SEQ, D_MODEL), 1.0),
        "x_sample": nrm(ks[2], (DEC_BATCH, DEC_SEQ, D_MODEL), 1.0),
        "p_prompt": nrm(ks[3], (DEPTH, BATCH, SEQ, PLE_DIM), 1.0),
        "p_sample": nrm(ks[4], (DEPTH, DEC_BATCH, DEC_SEQ, PLE_DIM), 1.0),
        "cache_k": nrm(ks[5], (DEPTH, n_phys, PAGE_SIZE, N_HEADS, HEAD_DIM), 1.0),
        "cache_v": nrm(ks[6], (DEPTH, n_phys, PAGE_SIZE, N_HEADS, HEAD_DIM), 1.0),
        "state_ssm": nrm(ks[7], (DEPTH, DEC_BATCH, N_GROUPS, SSM_STATE, 2), 0.1),
        "page_table": page_table,
        "g_mix": 1.0 + nrm(ks[8], (DEPTH, D_MODEL), 0.02),
        "w_in": nrm(ks[9], (DEPTH, D_MODEL, IN_WIDTH), D_MODEL ** -0.5),
        "g_q": 1.0 + nrm(ks[10], (DEPTH, HEAD_DIM), 0.02),
        "g_k": 1.0 + nrm(ks[11], (DEPTH, HEAD_DIM), 0.02),
        "sb_bias": SB_BIAS_INIT + nrm(ks[30], (DEPTH, N_HEADS), 0.1),
        "w_attn_out": nrm(ks[12], (DEPTH, ATT_WIDTH, D_MODEL), ATT_WIDTH ** -0.5),
        "ssm_a_re": -0.5 + nrm(ks[13], (DEPTH, N_GROUPS, SSM_STATE), 0.01),
        "ssm_a_im": math.pi * n_idx + nrm(ks[14], (DEPTH, N_GROUPS, SSM_STATE), 0.01),
        "ssm_log_dt": jax.random.uniform(ks[15], (DEPTH, N_GROUPS), f32, math.log(DT_MIN), math.log(DT_MAX)),
        "ssm_b_re": nrm(ks[16], (DEPTH, N_GROUPS, SSM_STATE, GROUP_CH), (2 * GROUP_CH) ** -0.5),
        "ssm_b_im": nrm(ks[17], (DEPTH, N_GROUPS, SSM_STATE, GROUP_CH), (2 * GROUP_CH) ** -0.5),
        "ssm_c_re": nrm(ks[18], (DEPTH, N_GROUPS, GROUP_CH, SSM_STATE), (2 * SSM_STATE) ** -0.5),
        "ssm_c_im": nrm(ks[19], (DEPTH, N_GROUPS, GROUP_CH, SSM_STATE), (2 * SSM_STATE) ** -0.5),
        "ssm_d": nrm(ks[20], (DEPTH, D_SSM), 1.0),
        "w_glu_val": nrm(ks[21], (DEPTH, D_SSM, D_MODEL), D_SSM ** -0.5),
        "w_glu_gate": nrm(ks[22], (DEPTH, D_SSM, D_MODEL), D_SSM ** -0.5),
        "w_out": nrm(ks[23], (DEPTH, D_MODEL, D_MODEL), D_MODEL ** -0.5),
        "g_mlp": 1.0 + nrm(ks[24], (DEPTH, D_MODEL), 0.02),
        "w_up": nrm(ks[25], (DEPTH, D_MODEL, D_FF), D_MODEL ** -0.5),
        "w_down": nrm(ks[26], (DEPTH, D_FF, D_MODEL), D_FF ** -0.5),
        "g_ple": 1.0 + nrm(ks[27], (DEPTH, D_MODEL), 0.02),
        "w_ple": nrm(ks[28], (DEPTH, PLE_DIM, D_MODEL), PLE_DIM ** -0.5),
        "w_ple_gate": nrm(ks[29], (DEPTH, D_MODEL, D_MODEL), D_MODEL ** -0.5),
    }


def reference(x_prompt, x_sample, p_prompt, p_sample, cache_k, cache_v, state_ssm, page_table,
              g_mix, w_in, g_q, g_k, sb_bias, w_attn_out, ssm_a_re, ssm_a_im, ssm_log_dt,
              ssm_b_re, ssm_b_im, ssm_c_re, ssm_c_im, ssm_d, w_glu_val, w_glu_gate, w_out,
              g_mlp, w_up, w_down, g_ple, w_ple, w_ple_gate):
    dec_batch, n_pages = page_table.shape
    past_len = n_pages * cache_k.shape[2]
    xp, xs = x_prompt, x_sample
    h0_prompt = jnp.zeros((x_prompt.shape[0], N_GROUPS, SSM_STATE, 2), state_ssm.dtype)
    kp_l, vp_l, sp_l, ks_l, vs_l, ss_l = [], [], [], [], [], []
    for i in range(DEPTH):
        w = (g_mix[i], w_in[i], g_q[i], g_k[i], sb_bias[i], w_attn_out[i],
             ssm_a_re[i], ssm_a_im[i], ssm_log_dt[i], ssm_b_re[i], ssm_b_im[i],
             ssm_c_re[i], ssm_c_im[i], ssm_d[i], w_glu_val[i], w_glu_gate[i], w_out[i],
             g_mlp[i], w_up[i], w_down[i], g_ple[i], w_ple[i], w_ple_gate[i])
        xp, kp, vp, hp = decoder_layer(xp, p_prompt[i], h0_prompt, sb_prompt, *w)
        past_k = cache_k[i][page_table].reshape(dec_batch, past_len, N_HEADS, HEAD_DIM)
        past_v = cache_v[i][page_table].reshape(dec_batch, past_len, N_HEADS, HEAD_DIM)
        xs, kss, vss, hs = decoder_layer(xs, p_sample[i], state_ssm[i],
                                         make_sample_attend(past_k, past_v), *w)
        kp_l.append(kp); vp_l.append(vp); sp_l.append(hp)
        ks_l.append(kss); vs_l.append(vss); ss_l.append(hs)
    k_prompt = jnp.stack(kp_l); v_prompt = jnp.stack(vp_l); ssm_prompt = jnp.stack(sp_l)
    k_sample = jnp.stack(ks_l); v_sample = jnp.stack(vs_l); ssm_sample = jnp.stack(ss_l)
    return (xp, xs, k_prompt, v_prompt, ssm_prompt, k_sample, v_sample, ssm_sample)
```

```python
import functools

import jax
import jax.numpy as jnp
from jax import lax
from jax.experimental import pallas as pl
from jax.experimental.pallas import tpu as pltpu

F32 = jnp.float32
BF16 = jnp.bfloat16

N_HEADS = 8
HEAD_DIM = 64
ATT_WIDTH = N_HEADS * HEAD_DIM
GROUP_CH = 16
N_GROUPS = 32
SSM_STATE = 64
D_SSM = N_GROUPS * GROUP_CH
RMS_EPS = 1e-6
LANES = 128
SUBLANES = 8
VMEM_LIMIT = 56 * 1024 * 1024

PROMPT_CHUNK = 16
ATT_TQ = 256
ATT_TK = 256
ROW_TILE = 256


def _dot(a, b):
    return jnp.dot(a, b, preferred_element_type=F32)


def _dot_nt(a, b):
    return lax.dot_general(a, b, (((1,), (1,)), ((), ())), preferred_element_type=F32)


def _split(x):
    hi = x.astype(BF16)
    lo = (x - hi.astype(F32)).astype(BF16)
    return hi, lo


def _dot3(a, b):
    ah, al = _split(a)
    bh, bl = _split(b)
    return _dot(ah, bh) + _dot(al, bh) + _dot(ah, bl)


def _params(sem):
    return pltpu.CompilerParams(dimension_semantics=sem, vmem_limit_bytes=VMEM_LIMIT)


def _resident(shape):
    nd = len(shape)
    return pl.BlockSpec(shape, lambda *_: (0,) * nd, pipeline_mode=pl.Buffered(1))


def _in_proj_kernel(x_ref, gmix_ref, w_ref, bd_ref, gq_ref, gk_ref,
                    q_ref, k_ref, v_ref, u_ref, ga_ref, gs_ref):
    x = x_ref[...]
    ms = jnp.mean(x * x, axis=-1, keepdims=True)
    h = (x * lax.rsqrt(ms + RMS_EPS) * gmix_ref[...]).astype(BF16)

    def head_norm(p, g_ref):
        hi, lo = _split(p * p)
        msh = _dot(hi, bd_ref[...]) + _dot(lo, bd_ref[...])
        return p * lax.rsqrt(msh + RMS_EPS) * g_ref[...]

    a = ATT_WIDTH
    q_ref[...] = head_norm(_dot(h, w_ref[:, 0:a]), gq_ref)
    k_ref[...] = head_norm(_dot(h, w_ref[:, a:2 * a]), gk_ref)
    v_ref[...] = _dot(h, w_ref[:, 2 * a:3 * a])
    u_ref[...] = _dot(h, w_ref[:, 3 * a:3 * a + D_SSM])
    o = 3 * a + D_SSM
    d = ga_ref.shape[-1]
    ga_ref[...] = _dot(h, w_ref[:, o:o + d])
    gs_ref[...] = _dot(h, w_ref[:, o + d:o + 2 * d])


def _in_proj(x, g_mix, w_in_bf, bd, gq_t, gk_t):
    m, d = x.shape
    tm = min(ROW_TILE, m)
    row = lambda w: pl.BlockSpec((tm, w), lambda i: (i, 0))
    widths = (ATT_WIDTH, ATT_WIDTH, ATT_WIDTH, D_SSM, d, d)
    return pl.pallas_call(
        _in_proj_kernel,
        grid=(m // tm,),
        in_specs=[row(d), _resident((1, d)), _resident(w_in_bf.shape), _resident(bd.shape),
                  _resident((1, ATT_WIDTH)), _resident((1, ATT_WIDTH))],
        out_specs=[row(w) for w in widths],
        out_shape=[jax.ShapeDtypeStruct((m, w), F32) for w in widths],
        compiler_params=_params(("parallel",)),
        name="in_proj",
    )(x, g_mix.reshape(1, d), w_in_bf, bd, gq_t, gk_t)


def _sb_block(z, mask, tri, carry):
    e = jnp.exp(-jnp.abs(z))
    sp = jnp.log(1.0 + e)
    ls_pos = jnp.minimum(z, 0.0) - sp
    ls_neg = ls_pos - z
    if mask is not None:
        ls_neg = jnp.where(mask, ls_neg, 0.0)
    hi, lo = _split(ls_neg)
    suffix = _dot(hi, tri) + _dot(lo, tri)
    w = jnp.exp(ls_pos + suffix + carry)
    if mask is not None:
        w = jnp.where(mask, w, 0.0)
    return w, carry + jnp.sum(ls_neg, axis=-1, keepdims=True)


def _tri(n):
    r = lax.broadcasted_iota(jnp.int32, (n, n), 0)
    c = lax.broadcasted_iota(jnp.int32, (n, n), 1)
    return jnp.where(r > c, 1.0, 0.0).astype(BF16)


def _attn_prompt_kernel(bias_ref, q_ref, k_ref, v_ref, o_ref, acc_ref, *, tq, tk):
    hp = pl.program_id(1)
    qi = pl.program_id(2)
    lane = lax.broadcasted_iota(jnp.int32, (1, LANES), 1)
    head_lanes = (lane < HEAD_DIM, lane >= HEAD_DIM)
    q = q_ref[...] * (HEAD_DIM ** -0.5)
    qh = [jnp.where(head_lanes[h], q, 0.0).astype(BF16) for h in range(2)]
    bias = [bias_ref[2 * hp + h] for h in range(2)]
    tri = _tri(tk)
    r = lax.broadcasted_iota(jnp.int32, (tq, tk), 0)
    c = lax.broadcasted_iota(jnp.int32, (tq, tk), 1)
    diag_mask = c < r
    acc_ref[...] = jnp.zeros_like(acc_ref)

    def block(kb, carries, mask):
        start = pl.multiple_of(kb * tk, tk)
        kblk = k_ref[pl.ds(start, tk), :].astype(BF16)
        vblk = v_ref[pl.ds(start, tk), :]
        out = []
        contrib = None
        for h in range(2):
            z = _dot_nt(qh[h], kblk) + bias[h]
            w, new_carry = _sb_block(z, mask, tri, carries[h])
            vh = jnp.where(head_lanes[h], vblk, 0.0).astype(BF16)
            pv = _dot(w.astype(BF16), vh)
            contrib = pv if contrib is None else contrib + pv
            out.append(new_carry)
        acc_ref[...] += contrib
        return tuple(out)

    zero = jnp.zeros((tq, 1), F32)
    carries = block(qi, (zero, zero), diag_mask)
    lax.fori_loop(0, qi, lambda i, cs: block(qi - 1 - i, cs, None), carries)
    o_ref[...] = acc_ref[...]


def _attn_prompt(q, k, v, bias, batch, seq):
    tq, tk = ATT_TQ, ATT_TK
    nq = seq // tq
    kern = functools.partial(_attn_prompt_kernel, tq=tq, tk=tk)
    return pl.pallas_call(
        kern,
        grid=(batch, N_HEADS // 2, nq),
        in_specs=[pl.BlockSpec(memory_space=pltpu.SMEM),
                  pl.BlockSpec((tq, LANES), lambda b, hp, i: (b * nq + i, hp)),
                  pl.BlockSpec((seq, LANES), lambda b, hp, i: (b, hp)),
                  pl.BlockSpec((seq, LANES), lambda b, hp, i: (b, hp))],
        out_specs=pl.BlockSpec((tq, LANES), lambda b, hp, i: (b * nq + i, hp)),
        out_shape=jax.ShapeDtypeStruct(q.shape, F32),
        scratch_shapes=[pltpu.VMEM((tq, LANES), F32)],
        compiler_params=_params(("parallel", "parallel", "arbitrary")),
        name="attn_prompt",
    )(bias, q, k, v)


def _attn_sample_kernel(pt_ref, bias_ref, hmask_ref, nmask_ref, q_ref, kn_ref, vn_ref, *rest,
                        n_pages, t_new, page):
    del pt_ref
    k_pages = rest[:n_pages]
    v_pages = rest[n_pages:2 * n_pages]
    o_ref = rest[2 * n_pages]
    rows = N_HEADS * t_new
    same_head = hmask_ref[...] > 0.5
    q = q_ref[...] * (HEAD_DIM ** -0.5)
    q_bd = jnp.where(same_head, jnp.concatenate([q] * N_HEADS, axis=0), 0.0).astype(BF16)
    bias = bias_ref[...]
    tri = _tri(page)

    pad = jnp.zeros((page - t_new, ATT_WIDTH), F32)
    k_new = jnp.concatenate([kn_ref[...], pad], axis=0).astype(BF16)
    v_new = jnp.concatenate([vn_ref[...], pad], axis=0).astype(BF16)
    new_mask = nmask_ref[...] > 0.5
    z =_dot_nt(q_bd, k_new) + bias
    w, carry = _sb_block(z, new_mask, tri, jnp.zeros((rows, 1), F32))
    acc = _dot(w.astype(BF16), v_new)
    for j in reversed(range(n_pages)):
        z = _dot_nt(q_bd, k_pages[j][...].astype(BF16)) + bias
        w, carry = _sb_block(z, None, tri, carry)
        acc = acc + _dot(w.astype(BF16), v_pages[j][...].astype(BF16))
    acc = jnp.where(same_head, acc, 0.0)
    out = acc[0:t_new, :]
    for h in range(1, N_HEADS):
        out = out + acc[h * t_new:(h + 1) * t_new, :]
    o_ref[...] = out


def _attn_sample(q, k_new, v_new, cache_k, cache_v, layer, page_table, bias_rows, t_new):
    n_seq, n_pages = page_table.shape
    page = cache_k.shape[2]
    page_spec = lambda j: pl.BlockSpec((None, None, page, ATT_WIDTH),
                                       lambda b, pt, j=j: (layer, pt[b, j], 0, 0))
    tok = pl.BlockSpec((t_new, ATT_WIDTH), lambda b, pt: (b, 0))
    rows = N_HEADS * t_new
    whole = lambda w: pl.BlockSpec((rows, w), lambda b, pt: (0, 0))
    row_head = jnp.arange(rows, dtype=jnp.int32)[:, None] // t_new
    row_query = jnp.arange(rows, dtype=jnp.int32)[:, None] % t_new
    head_mask = (row_head == jnp.arange(ATT_WIDTH, dtype=jnp.int32)[None, :] // HEAD_DIM).astype(F32)
    new_mask = (jnp.arange(page, dtype=jnp.int32)[None, :] < row_query).astype(F32)
    kern = functools.partial(_attn_sample_kernel, n_pages=n_pages, t_new=t_new, page=page)
    grid_spec = pltpu.PrefetchScalarGridSpec(
        num_scalar_prefetch=1,
        grid=(n_seq,),
        in_specs=[whole(1), whole(ATT_WIDTH), whole(page), tok, tok, tok]
                 + [page_spec(j) for j in range(n_pages)] * 2,
        out_specs=tok,
    )
    return pl.pallas_call(
        kern,
        grid_spec=grid_spec,
        out_shape=jax.ShapeDtypeStruct(q.shape, F32),
        compiler_params=_params(("arbitrary",)),
        name="attn_sample",
    )(page_table, bias_rows, head_mask, new_mask, q, k_new, v_new,
      *([cache_k] * n_pages), *([cache_v] * n_pages))


def _ssm_kernel(u_ref, m_ref, wre_ref, wim_ref, vre_ref, vim_ref, a_ref, h0re_ref, h0im_ref,
                y_ref, hre_ref, him_ref, zre_s, zim_s, inre_s, inim_s, *, n_chunks, bp):
    u = u_ref[...]
    zre_s[...] = _dot3(u, wre_ref[...])
    zim_s[...] = _dot3(u, wim_ref[...])
    ar = a_ref[0:1, :]
    ai = a_ref[1:2, :]

    def step(k, carry):
        hr, hi = carry
        r0 = pl.multiple_of(k * bp, bp)
        inre_s[pl.ds(r0, bp), :] = hr
        inim_s[pl.ds(r0, bp), :] = hi
        zr = zre_s[pl.ds(r0, bp), :]
        zi = zim_s[pl.ds(r0, bp), :]
        return ar * hr - ai * hi + zr, ar * hi + ai * hr + zi

    hr, hi = lax.fori_loop(0, n_chunks, step, (h0re_ref[...], h0im_ref[...]))
    hre_ref[...] = hr
    him_ref[...] = hi
    y_ref[...] = (_dot3(u, m_ref[...]) + _dot3(inre_s[...], vre_ref[...])
                  + _dot3(inim_s[...], vim_ref[...]))


def _ssm_mats(a_re, a_im, log_dt, b_re, b_im, c_re, c_im, L):
    hp = lax.Precision.HIGHEST
    dt = jnp.exp(log_dt)[:, None]
    mag = jnp.exp(a_re * dt)
    ab_re, ab_im = mag * jnp.cos(a_im * dt), mag * jnp.sin(a_im * dt)
    den = a_re * a_re + a_im * a_im
    nr, ni = ab_re - 1.0, ab_im
    f_re = (nr * a_re + ni * a_im) / den
    f_im = (ni * a_re - nr * a_im) / den
    bb_re = f_re[..., None] * b_re - f_im[..., None] * b_im
    bb_im = f_re[..., None] * b_im + f_im[..., None] * b_re
    pr, pi = [jnp.ones_like(ab_re)], [jnp.zeros_like(ab_re)]
    for _ in range(L):
        pr, pi = pr + [ab_re * pr[-1] - ab_im * pi[-1]], pi + [ab_re * pi[-1] + ab_im * pr[-1]]
    p_re, p_im = jnp.stack(pr), jnp.stack(pi)
    abr = p_re[:L, :, :, None] * bb_re - p_im[:L, :, :, None] * bb_im
    abi = p_re[:L, :, :, None] * bb_im + p_im[:L, :, :, None] * bb_re
    kern = (jnp.einsum('gcp,tgpd->tgcd', c_re, abr, precision=hp)
            - jnp.einsum('gcp,tgpd->tgcd', c_im, abi, precision=hp))
    s = jnp.arange(L)[:, None]
    t = jnp.arange(L)[None, :]
    tau = t - s
    m = jnp.where((tau >= 0)[:, :, None, None, None], kern[jnp.clip(tau, 0, L - 1)], 0.0)
    m = jnp.transpose(m, (2, 0, 4, 1, 3)).reshape(N_GROUPS, L * GROUP_CH, L * GROUP_CH)
    lc = L * GROUP_CH
    w_re = jnp.transpose(abr[::-1], (1, 0, 3, 2)).reshape(N_GROUPS, lc, SSM_STATE)
    w_im = jnp.transpose(abi[::-1], (1, 0, 3, 2)).reshape(N_GROUPS, lc, SSM_STATE)
    cr = jnp.transpose(c_re, (0, 2, 1))[:, :, None, :]
    ci = jnp.transpose(c_im, (0, 2, 1))[:, :, None, :]
    qr = jnp.transpose(p_re[1:], (1, 2, 0))[..., None]
    qi = jnp.transpose(p_im[1:], (1, 2, 0))[..., None]
    v_re = (cr * qr - ci * qi).reshape(N_GROUPS, SSM_STATE, lc)
    v_im = (-cr * qi - ci * qr).reshape(N_GROUPS, SSM_STATE, lc)
    a_l = jnp.stack([p_re[L], p_im[L]], axis=1)
    return m, w_re, w_im, v_re, v_im, a_l


def _ssm(u, h0, mats, batch, seq, L):
    m, w_re, w_im, v_re, v_im, a_l = mats
    n_chunks = seq // L
    bp = -(-batch // SUBLANES) * SUBLANES
    lc = L * GROUP_CH
    rows = n_chunks * bp
    ug = u.reshape(batch, n_chunks, L, N_GROUPS, GROUP_CH)
    ug = jnp.transpose(ug, (3, 1, 0, 2, 4))
    ug = jnp.pad(ug, ((0, 0), (0, 0), (0, bp - batch), (0, 0), (0, 0))).reshape(N_GROUPS, rows, lc)
    if h0 is None:
        h0re = jnp.zeros((N_GROUPS, bp, SSM_STATE), F32)
        h0im = h0re
    else:
        h0t = jnp.pad(jnp.transpose(h0, (1, 0, 2, 3)), ((0, 0), (0, bp - batch), (0, 0), (0, 0)))
        h0re, h0im = h0t[..., 0], h0t[..., 1]
    grp = lambda a, b: pl.BlockSpec((None, a, b), lambda g: (g, 0, 0))
    kern = functools.partial(_ssm_kernel, n_chunks=n_chunks, bp=bp)
    y, hre, him = pl.pallas_call(
        kern,
        grid=(N_GROUPS,),
        in_specs=[grp(rows, lc), grp(lc, lc), grp(lc, SSM_STATE), grp(lc, SSM_STATE),
                  grp(SSM_STATE, lc), grp(SSM_STATE, lc), grp(2, SSM_STATE),
                  grp(bp, SSM_STATE), grp(bp, SSM_STATE)],
        out_specs=[grp(rows, lc), grp(bp, SSM_STATE), grp(bp, SSM_STATE)],
        out_shape=[jax.ShapeDtypeStruct((N_GROUPS, rows, lc), F32),
                   jax.ShapeDtypeStruct((N_GROUPS, bp, SSM_STATE), F32),
                   jax.ShapeDtypeStruct((N_GROUPS, bp, SSM_STATE), F32)],
        scratch_shapes=[pltpu.VMEM((rows, SSM_STATE), F32)] * 4,
        compiler_params=_params(("parallel",)),
        name="ssm",
    )(ug, m, w_re, w_im, v_re, v_im, a_l, h0re, h0im)
    y = y.reshape(N_GROUPS, n_chunks, bp, L, GROUP_CH)[:, :, :batch]
    y = jnp.transpose(y, (2, 1, 3, 0, 4)).reshape(batch * seq, D_SSM)
    h_last = jnp.transpose(jnp.stack([hre, him], axis=-1)[:, :batch], (1, 0, 2, 3))
    return y, h_last


def _post_kernel(x_ref, o_ref, y_ref, u_ref, ga_ref, gs_ref, p_ref,
                 d_ref, wao_ref, wgv_ref, wgg_ref, wout_ref, gmlp_ref, wup_ref, wdown_ref,
                 gple_ref, wple_ref, wpg_ref, out_ref, *, ff_chunk):
    def rms(x, g_ref):
        ms = jnp.mean(x * x, axis=-1, keepdims=True)
        return (x * lax.rsqrt(ms + RMS_EPS) * g_ref[...]).astype(BF16)

    y_act = jax.nn.gelu(y_ref[...] + d_ref[...] * u_ref[...], approximate=True).astype(BF16)
    attn_out = _dot(o_ref[...].astype(BF16), wao_ref[...])
    ssm_out = _dot(y_act, wgv_ref[...]) * jax.nn.sigmoid(_dot(y_act, wgg_ref[...]))
    merged = jax.nn.sigmoid(ga_ref[...]) * attn_out + jax.nn.sigmoid(gs_ref[...]) * ssm_out
    x = x_ref[...] + _dot(merged.astype(BF16), wout_ref[...])
    h2 = rms(x, gmlp_ref)
    d_ff = wup_ref.shape[1]
    mlp = None
    for c0 in range(0, d_ff, ff_chunk):
        hid = jnp.square(jnp.maximum(_dot(h2, wup_ref[:, c0:c0 + ff_chunk]), 0.0)).astype(BF16)
        part = _dot(hid, wdown_ref[c0:c0 + ff_chunk, :])
        mlp = part if mlp is None else mlp + part
    x = x + mlp
    h3 = rms(x, gple_ref)
    out_ref[...] = x + _dot(p_ref[...].astype(BF16), wple_ref[...]) * jax.nn.sigmoid(_dot(h3, wpg_ref[...]))


def _post(x, o, y, u, ga, gs, p, d_skip, w):
    m, d = x.shape
    tm = min(ROW_TILE, m)
    row = lambda a: pl.BlockSpec((tm, a.shape[1]), lambda i: (i, 0))
    vec = lambda a: a.reshape(1, -1)
    acts = (x, o, y, u, ga, gs, p)
    consts = (vec(d_skip), w["w_attn_out"], w["w_glu_val"], w["w_glu_gate"], w["w_out"], vec(w["g_mlp"]),
              w["w_up"], w["w_down"], vec(w["g_ple"]), w["w_ple"], w["w_ple_gate"])
    return pl.pallas_call(
        functools.partial(_post_kernel, ff_chunk=1024),
        grid=(m // tm,),
        in_specs=[row(a) for a in acts] + [_resident(c.shape) for c in consts],
        out_specs=pl.BlockSpec((tm, d), lambda i: (i, 0)),
        out_shape=jax.ShapeDtypeStruct((m, d), F32),
        compiler_params=_params(("parallel",)),
        name="post",
    )(*acts, *consts)


def _layer(x, p_l, attend, h0, batch, seq, chunk, w):
    m = batch * seq
    q, k, v, u, ga, gs = _in_proj(x.reshape(m, -1), w["g_mix"], w["w_in"], w["bd"], w["g_q"], w["g_k"])
    o = attend(q, k, v)
    y, h_last = _ssm(u, h0, w["ssm_mats"][chunk], batch, seq, chunk)
    x_out = _post(x.reshape(m, -1), o, y, u, ga, gs, p_l.reshape(m, -1), w["ssm_d"], w)
    shape4 = (batch, seq, N_HEADS, HEAD_DIM)
    return x_out.reshape(x.shape), k.reshape(shape4), v.reshape(shape4), h_last


def kernel(x_prompt, x_sample, p_prompt, p_sample, cache_k, cache_v, state_ssm, page_table, g_mix, w_in, g_q, g_k, sb_bias, w_attn_out, ssm_a_re, ssm_a_im, ssm_log_dt, ssm_b_re, ssm_b_im, ssm_c_re, ssm_c_im, ssm_d, w_glu_val, w_glu_gate, w_out, g_mlp, w_up, w_down, g_ple, w_ple, w_ple_gate):
    depth = w_in.shape[0]
    batch, seq, _ = x_prompt.shape
    dec_batch, dec_seq, _ = x_sample.shape
    ck = cache_k.reshape(cache_k.shape[:3] + (ATT_WIDTH,))
    cv = cache_v.reshape(cache_v.shape[:3] + (ATT_WIDTH,))
    hd = jnp.arange(ATT_WIDTH, dtype=jnp.int32) // HEAD_DIM
    bd = jnp.where(hd[:, None] == hd[None, :], 1.0 / HEAD_DIM, 0.0).astype(BF16)

    xp, xs = x_prompt, x_sample
    outs = [[] for _ in range(6)]
    for i in range(depth):
        ssm_p = (ssm_a_re[i], ssm_a_im[i], ssm_log_dt[i], ssm_b_re[i], ssm_b_im[i], ssm_c_re[i], ssm_c_im[i])
        w = {
            "g_mix": g_mix[i], "w_in": w_in[i].astype(BF16), "bd": bd,
            "g_q": jnp.tile(g_q[i], N_HEADS).reshape(1, ATT_WIDTH),
            "g_k": jnp.tile(g_k[i], N_HEADS).reshape(1, ATT_WIDTH),
            "w_attn_out": w_attn_out[i].astype(BF16), "w_glu_val": w_glu_val[i].astype(BF16),
            "w_glu_gate": w_glu_gate[i].astype(BF16), "w_out": w_out[i].astype(BF16),
            "g_mlp": g_mlp[i], "w_up": w_up[i].astype(BF16), "w_down": w_down[i].astype(BF16),
            "g_ple": g_ple[i], "w_ple": w_ple[i].astype(BF16), "w_ple_gate": w_ple_gate[i].astype(BF16),
            "ssm_d": ssm_d[i],
            "ssm_mats": {L: _ssm_mats(*ssm_p, L) for L in sorted({PROMPT_CHUNK, dec_seq})},
        }
        bias = sb_bias[i].astype(F32)
        bias_rows = jnp.repeat(bias, dec_seq).reshape(N_HEADS * dec_seq, 1)

        attend_p = lambda q, k, v: _attn_prompt(q, k, v, bias, batch, seq)
        xp, kp, vp, hp = _layer(xp, p_prompt[i], attend_p, None, batch, seq, PROMPT_CHUNK, w)
        attend_s = lambda q, k, v: _attn_sample(q, k, v, ck, cv, i, page_table, bias_rows, dec_seq)
        xs, ks, vs, hs = _layer(xs, p_sample[i], attend_s, state_ssm[i], dec_batch, dec_seq, dec_seq, w)
        for lst, val in zip(outs, (kp, vp, hp, ks, vs, hs)):
            lst.append(val)
    k_p, v_p, s_p, k_s, v_s, s_s = (jnp.stack(o) for o in outs)
    return xp, xs, k_p, v_p, s_p, k_s, v_s, s_s
```

```python
import functools

import jax
import jax.numpy as jnp
from jax import lax
from jax.experimental import pallas as pl
from jax.experimental.pallas import tpu as pltpu

F32 = jnp.float32
BF16 = jnp.bfloat16

N_HEADS = 8
HEAD_DIM = 64
ATT_WIDTH = N_HEADS * HEAD_DIM
GROUP_CH = 16
N_GROUPS = 32
SSM_STATE = 64
D_SSM = N_GROUPS * GROUP_CH
RMS_EPS = 1e-6
LOG2E = 1.4426950408889634
LANES = 128
GROUPS_PER_BLOCK = LANES // GROUP_CH
VMEM_LIMIT = 56 * 1024 * 1024

PROMPT_CHUNK = 16
ATT_TK = 256
ATT_TQ = 2 * ATT_TK
ROW_TILE = 256


def _dot(a, b):
    return jnp.dot(a, b, preferred_element_type=F32)


def _dot_nt(a, b):
    return lax.dot_general(a, b, (((1,), (1,)), ((), ())), preferred_element_type=F32)


def _split(x):
    hi = x.astype(BF16)
    lo = (x - hi.astype(F32)).astype(BF16)
    return hi, lo


def _dot3(a, b):
    ah, al = _split(a)
    bh, bl = _split(b)
    return _dot(ah, bh) + _dot(al, bh) + _dot(ah, bl)


def _params(sem):
    return pltpu.CompilerParams(dimension_semantics=sem, vmem_limit_bytes=VMEM_LIMIT)


def _resident(shape):
    nd = len(shape)
    return pl.BlockSpec(shape, lambda *_: (0,) * nd, pipeline_mode=pl.Buffered(1))


def _in_proj_kernel(x_ref, gmix_ref, w_ref, bd_ref, gq_ref, gk_ref,
                    q_ref, k_ref, v_ref, u_ref, ga_ref, gs_ref):
    x = x_ref[...]
    ms = jnp.mean(x * x, axis=-1, keepdims=True)
    h = (x * lax.rsqrt(ms + RMS_EPS) * gmix_ref[...]).astype(BF16)

    def head_norm(p, g_ref):
        hi, lo = _split(p * p)
        msh = _dot(hi, bd_ref[...]) + _dot(lo, bd_ref[...])
        return p * lax.rsqrt(msh + RMS_EPS) * g_ref[...]

    a = ATT_WIDTH
    q_ref[...] = head_norm(_dot(h, w_ref[:, 0:a]), gq_ref)
    k_ref[...] = head_norm(_dot(h, w_ref[:, a:2 * a]), gk_ref)
    v_ref[...] = _dot(h, w_ref[:, 2 * a:3 * a])
    u_ref[...] = _dot(h, w_ref[:, 3 * a:3 * a + D_SSM])
    o = 3 * a + D_SSM
    d = ga_ref.shape[-1]
    ga_ref[...] = _dot(h, w_ref[:, o:o + d])
    gs_ref[...] = _dot(h, w_ref[:, o + d:o + 2 * d])


def _in_proj(x, g_mix, w_in_bf, bd, gq_t, gk_t):
    m, d = x.shape
    tm = min(ROW_TILE, m)
    row = lambda w: pl.BlockSpec((tm, w), lambda i: (i, 0))
    widths = (ATT_WIDTH, ATT_WIDTH, ATT_WIDTH, D_SSM, d, d)
    return pl.pallas_call(
        _in_proj_kernel,
        grid=(m // tm,),
        in_specs=[row(d), _resident((1, d)), _resident(w_in_bf.shape), _resident(bd.shape),
                  _resident((1, ATT_WIDTH)), _resident((1, ATT_WIDTH))],
        out_specs=[row(w) for w in widths],
        out_shape=[jax.ShapeDtypeStruct((m, w), F32) for w in widths],
        compiler_params=_params(("parallel",)),
        name="in_proj",
    )(x, g_mix.reshape(1, d), w_in_bf, bd, gq_t, gk_t)


def _sb_block(nz, mask, tri, carry):
    m = jnp.minimum(nz, 0.0)
    e = jnp.exp2(m + (m - nz))
    ls = m - jnp.log(1.0 + e) * LOG2E
    if mask is not None:
        ls = jnp.where(mask, ls, 0.0)
    if tri.shape[0] == tri.shape[1]:
        parts = ls.astype(BF16)
    else:
        parts = jnp.concatenate(_split(ls), axis=1)
    later = _dot(parts, tri)
    w = jnp.exp2((ls - nz) + (later + carry))
    if mask is not None:
        w = jnp.where(mask, w, 0.0)
    return w, carry + (later[:, 0:1] + ls[:, 0:1])


def _later_keys(n, copies):
    assert copies in (1, 2)
    r = lax.broadcasted_iota(jnp.int32, (copies * n, n), 0)
    c = lax.broadcasted_iota(jnp.int32, (copies * n, n), 1)
    return jnp.where(jnp.where(r >= n, r - n, r) > c, 1.0, 0.0).astype(BF16)


def _attn_prompt_kernel(bias_ref, q_ref, k_ref, v_ref, o_ref, kb_s, vb_s, acc_ref, carry_ref,
                        *, tq, tk, seq):
    hp = pl.program_id(1)
    qi = pl.program_id(2)
    lane = lax.broadcasted_iota(jnp.int32, (1, LANES), 1)
    first_head = lane < HEAD_DIM

    @pl.when(qi == 0)
    def _stage_keys_values():
        def stage(i, _):
            rows = pl.ds(pl.multiple_of(i * tk, tk), tk)
            kb_s[rows, :] = k_ref[rows, :].astype(BF16)
            vb_s[rows, :] = v_ref[rows, :].astype(BF16)
            return 0
        lax.fori_loop(0, seq // tk, stage, 0)

    qs = q_ref[...] * (-(HEAD_DIM ** -0.5) * LOG2E)
    q2 = jnp.concatenate([jnp.where(first_head, qs, 0.0), jnp.where(first_head, 0.0, qs)], axis=0).astype(BF16)
    nbias = jnp.concatenate([jnp.full((tq, 1), bias_ref[2 * hp + h] * (-LOG2E), F32) for h in range(2)], axis=0)
    tri = _later_keys(tk, 1)
    r = lax.broadcasted_iota(jnp.int32, (2 * tq, tk), 0)
    c = lax.broadcasted_iota(jnp.int32, (2 * tq, tk), 1)
    q_pos = jnp.where(r < tq, r, r - tq) + qi * tq

    def block(kb, mask, carry):
        rows = pl.ds(pl.multiple_of(kb * tk, tk), tk)
        nz = _dot_nt(q2, kb_s[rows, :]) + nbias
        w, carry = _sb_block(nz, mask, tri, carry)
        pv = _dot(w.astype(BF16), vb_s[rows, :])
        return jnp.where(first_head, pv[:tq], pv[tq:]), carry

    def run(blocks, masked, carry):
        total = None
        for kb in blocks:
            contrib, carry = block(kb, (c + kb * tk < q_pos) if masked else None, carry)
            total = contrib if total is None else total + contrib
        return total, carry

    per_q = tq // tk
    top = per_q * (qi + 1) - 1
    total, carry = run([top - d for d in range(per_q)], True, jnp.zeros((2 * tq, 1), F32))
    acc_ref[...] = total
    carry_ref[...] = carry

    def below_diagonal(i, _):
        first = top - per_q * (i + 1)
        total, carry = run([first - d for d in range(per_q)], False, carry_ref[...])
        acc_ref[...] += total
        carry_ref[...] = carry
        return 0
    lax.fori_loop(0, qi, below_diagonal, 0)
    o_ref[...] = acc_ref[...]


def _attn_prompt(q, k, v, bias, batch, seq):
    tq, tk = ATT_TQ, ATT_TK
    nq = seq // tq
    kern = functools.partial(_attn_prompt_kernel, tq=tq, tk=tk, seq=seq)
    return pl.pallas_call(
        kern,
        grid=(batch, N_HEADS // 2, nq),
        in_specs=[pl.BlockSpec(memory_space=pltpu.SMEM),
                  pl.BlockSpec((tq, LANES), lambda b, hp, i: (b * nq + i, hp)),
                  pl.BlockSpec((seq, LANES), lambda b, hp, i: (b, hp)),
                  pl.BlockSpec((seq, LANES), lambda b, hp, i: (b, hp))],
        out_specs=pl.BlockSpec((tq, LANES), lambda b, hp, i: (b * nq + i, hp)),
        out_shape=jax.ShapeDtypeStruct(q.shape, F32),
        scratch_shapes=[pltpu.VMEM((seq, LANES), BF16)] * 2
                       + [pltpu.VMEM((tq, LANES), F32), pltpu.VMEM((2 * tq, 1), F32)],
        compiler_params=_params(("parallel", "parallel", "arbitrary")),
        name="attn_prompt",
    )(bias, q, k, v)


def _attn_sample_kernel(pt_ref, bias_ref, hmask_ref, nmask_ref, q_ref, kn_ref, vn_ref, *rest,
                        n_pages, t_new, page):
    del pt_ref
    k_pages = rest[:n_pages]
    v_pages = rest[n_pages:2 * n_pages]
    o_ref = rest[2 * n_pages]
    rows = N_HEADS * t_new
    same_head = hmask_ref[...] > 0.5
    q = q_ref[...] * (-(HEAD_DIM ** -0.5) * LOG2E)
    q_bd = jnp.where(same_head, jnp.concatenate([q] * N_HEADS, axis=0), 0.0).astype(BF16)
    nbias = bias_ref[...] * (-LOG2E)
    tri = _later_keys(page, 2)

    def heads_to_lanes(page_ref):
        per_head = [page_ref[pl.ds(h, page, stride=N_HEADS), :] for h in range(N_HEADS)]
        return jnp.concatenate(per_head, axis=1).astype(BF16)

    pad = jnp.zeros((page - t_new, ATT_WIDTH), F32)
    k_new = jnp.concatenate([kn_ref[...], pad], axis=0).astype(BF16)
    v_new = jnp.concatenate([vn_ref[...], pad], axis=0).astype(BF16)
    new_mask = nmask_ref[...] > 0.5
    nz = _dot_nt(q_bd, k_new) + nbias
    w, carry = _sb_block(nz, new_mask, tri, jnp.zeros((rows, 1), F32))
    acc = _dot(w.astype(BF16), v_new)
    for j in reversed(range(n_pages)):
        nz = _dot_nt(q_bd, heads_to_lanes(k_pages[j])) + nbias
        w, carry = _sb_block(nz, None, tri, carry)
        acc = acc + _dot(w.astype(BF16), heads_to_lanes(v_pages[j]))
    acc = jnp.where(same_head, acc, 0.0)
    out = acc[0:t_new, :]
    for h in range(1, N_HEADS):
        out = out + acc[h * t_new:(h + 1) * t_new, :]
    o_ref[...] = out


def _attn_sample(q, k_new, v_new, cache_k, cache_v, layer, page_table, bias_rows, t_new):
    n_seq, n_pages = page_table.shape
    page = cache_k.shape[2] // N_HEADS
    page_spec = lambda j: pl.BlockSpec((None, None, page * N_HEADS, HEAD_DIM),
                                       lambda b, pt, j=j: (layer, pt[b, j], 0, 0))
    tok = pl.BlockSpec((t_new, ATT_WIDTH), lambda b, pt: (b, 0))
    rows = N_HEADS * t_new
    whole = lambda w: pl.BlockSpec((rows, w), lambda b, pt: (0, 0))
    row_head = jnp.arange(rows, dtype=jnp.int32)[:, None] // t_new
    row_query = jnp.arange(rows, dtype=jnp.int32)[:, None] % t_new
    head_mask = (row_head == jnp.arange(ATT_WIDTH, dtype=jnp.int32)[None, :] // HEAD_DIM).astype(F32)
    new_mask = (jnp.arange(page, dtype=jnp.int32)[None, :] < row_query).astype(F32)
    kern = functools.partial(_attn_sample_kernel, n_pages=n_pages, t_new=t_new, page=page)
    grid_spec = pltpu.PrefetchScalarGridSpec(
        num_scalar_prefetch=1,
        grid=(n_seq,),
        in_specs=[whole(1), whole(ATT_WIDTH), whole(page), tok, tok, tok]
                 + [page_spec(j) for j in range(n_pages)] * 2,
        out_specs=tok,
    )
    return pl.pallas_call(
        kern,
        grid_spec=grid_spec,
        out_shape=jax.ShapeDtypeStruct(q.shape, F32),
        compiler_params=_params(("arbitrary",)),
        name="attn_sample",
    )(page_table, bias_rows, head_mask, new_mask, q, k_new, v_new,
      *([cache_k] * n_pages), *([cache_v] * n_pages))


def _ssm_kernel(u_ref, m_ref, wre_ref, wim_ref, vre_ref, vim_ref, a_ref, h0re_ref, h0im_ref,
                y_ref, hre_ref, him_ref, ug_s, yg_s, zre_s, zim_s, inre_s, inim_s, *, L, n_steps, bp):
    n_rows = n_steps * bp
    p = SSM_STATE
    lane = lax.broadcasted_iota(jnp.int32, (1, LANES), 1)
    in_slot = [(lane >= j * GROUP_CH) & (lane < (j + 1) * GROUP_CH) for j in range(GROUPS_PER_BLOCK)]

    def regroup(pieces, dst):
        out = None
        for j, piece in enumerate(pieces):
            shift = ((j - dst) * GROUP_CH) % LANES
            moved = pltpu.roll(piece, shift, 1) if shift else piece
            out = moved if out is None else jnp.where(in_slot[j], moved, out)
        return out

    xs = [u_ref[pl.ds(s, n_rows, stride=L), :] for s in range(L)]
    n_half = L // GROUPS_PER_BLOCK
    for g in range(GROUPS_PER_BLOCK):
        halves = [regroup(xs[h * GROUPS_PER_BLOCK:(h + 1) * GROUPS_PER_BLOCK], g) for h in range(n_half)]
        u_g = halves[0] if n_half == 1 else jnp.concatenate(halves, axis=1)
        ug_s[g] = u_g
        zre_s[:, g * p:(g + 1) * p] = _dot3(u_g, wre_ref[g])
        zim_s[:, g * p:(g + 1) * p] = _dot3(u_g, wim_ref[g])

    ar = a_ref[0:1, :]
    ai = a_ref[1:2, :]

    def step(k, carry):
        hr, hi = carry
        rows = pl.ds(k * bp, bp)
        inre_s[rows, :] = hr
        inim_s[rows, :] = hi
        zr = zre_s[rows, :]
        zi = zim_s[rows, :]
        return ar * hr - ai * hi + zr, ar * hi + ai * hr + zi

    hr, hi = lax.fori_loop(0, n_steps, step, (h0re_ref[...], h0im_ref[...]))
    hre_ref[...] = hr
    him_ref[...] = hi

    for g in range(GROUPS_PER_BLOCK):
        yg_s[g] = (_dot3(ug_s[g], m_ref[g]) + _dot3(inre_s[:, g * p:(g + 1) * p], vre_ref[g])
                   + _dot3(inim_s[:, g * p:(g + 1) * p], vim_ref[g]))
    for t in range(L):
        h, j = divmod(t, GROUPS_PER_BLOCK)
        pieces = [yg_s[g, :, h * LANES:(h + 1) * LANES] for g in range(GROUPS_PER_BLOCK)]
        y_ref[pl.ds(t, n_rows, stride=L), :] = regroup(pieces, j)


def _ssm_mats(a_re, a_im, log_dt, b_re, b_im, c_re, c_im, L):
    hp = lax.Precision.HIGHEST
    dt = jnp.exp(log_dt)[:, None]
    mag = jnp.exp(a_re * dt)
    ab_re, ab_im = mag * jnp.cos(a_im * dt), mag * jnp.sin(a_im * dt)
    den = a_re * a_re + a_im * a_im
    nr, ni = ab_re - 1.0, ab_im
    f_re = (nr * a_re + ni * a_im) / den
    f_im = (ni * a_re - nr * a_im) / den
    bb_re = f_re[..., None] * b_re - f_im[..., None] * b_im
    bb_im = f_re[..., None] * b_im + f_im[..., None] * b_re
    pr, pi = [jnp.ones_like(ab_re)], [jnp.zeros_like(ab_re)]
    for _ in range(L):
        pr, pi = pr + [ab_re * pr[-1] - ab_im * pi[-1]], pi + [ab_re * pi[-1] + ab_im * pr[-1]]
    p_re, p_im = jnp.stack(pr), jnp.stack(pi)
    abr = p_re[:L, :, :, None] * bb_re - p_im[:L, :, :, None] * bb_im
    abi = p_re[:L, :, :, None] * bb_im + p_im[:L, :, :, None] * bb_re
    kern = (jnp.einsum('gcp,tgpd->tgcd', c_re, abr, precision=hp)
            - jnp.einsum('gcp,tgpd->tgcd', c_im, abi, precision=hp))
    s = jnp.arange(L)[:, None]
    t = jnp.arange(L)[None, :]
    tau = t - s
    m = jnp.where((tau >= 0)[:, :, None, None, None], kern[jnp.clip(tau, 0, L - 1)], 0.0)
    m = jnp.transpose(m, (2, 0, 4, 1, 3)).reshape(N_GROUPS, L * GROUP_CH, L * GROUP_CH)
    lc = L * GROUP_CH
    w_re = jnp.transpose(abr[::-1], (1, 0, 3, 2)).reshape(N_GROUPS, lc, SSM_STATE)
    w_im = jnp.transpose(abi[::-1], (1, 0, 3, 2)).reshape(N_GROUPS, lc, SSM_STATE)
    cr = jnp.transpose(c_re, (0, 2, 1))[:, :, None, :]
    ci = jnp.transpose(c_im, (0, 2, 1))[:, :, None, :]
    qr = jnp.transpose(p_re[1:], (1, 2, 0))[..., None]
    qi = jnp.transpose(p_im[1:], (1, 2, 0))[..., None]
    v_re = (cr * qr - ci * qi).reshape(N_GROUPS, SSM_STATE, lc)
    v_im = (-cr * qi - ci * qr).reshape(N_GROUPS, SSM_STATE, lc)
    a_l = jnp.stack([p_re[L], p_im[L]], axis=1)
    return m, w_re, w_im, v_re, v_im, a_l


def _ssm(u, h0, mats, batch, seq, L):
    m, w_re, w_im, v_re, v_im, a_l = mats
    gpb = GROUPS_PER_BLOCK
    n_cb = N_GROUPS // gpb
    lc = L * GROUP_CH
    state_w = gpb * SSM_STATE
    if seq == L:
        n_bblk, bp, n_steps = 1, batch, 1
    else:
        n_bblk, bp, n_steps = batch, 1, seq // L
    tok = bp * n_steps * L
    n_rows = bp * n_steps

    def to_blocks(x):
        x = x.reshape(n_bblk, bp, n_cb, state_w)
        return jnp.transpose(x, (0, 2, 1, 3))

    if h0 is None:
        h0re = h0im = jnp.zeros((n_bblk, n_cb, bp, state_w), F32)
    else:
        h0re, h0im = to_blocks(h0[..., 0]), to_blocks(h0[..., 1])
    a_blk = jnp.transpose(a_l.reshape(n_cb, gpb, 2, SSM_STATE), (0, 2, 1, 3)).reshape(n_cb, 2, state_w)

    grp = lambda a, b: pl.BlockSpec((gpb, a, b), lambda i, cb: (cb, 0, 0))
    act = pl.BlockSpec((tok, LANES), lambda i, cb: (i, cb))
    state = pl.BlockSpec((None, None, bp, state_w), lambda i, cb: (i, cb, 0, 0))
    state_shape = jax.ShapeDtypeStruct((n_bblk, n_cb, bp, state_w), F32)
    kern = functools.partial(_ssm_kernel, L=L, n_steps=n_steps, bp=bp)
    y, hre, him = pl.pallas_call(
        kern,
        grid=(n_bblk, n_cb),
        in_specs=[act, grp(lc, lc), grp(lc, SSM_STATE), grp(lc, SSM_STATE),
                  grp(SSM_STATE, lc), grp(SSM_STATE, lc),
                  pl.BlockSpec((None, 2, state_w), lambda i, cb: (cb, 0, 0)), state, state],
        out_specs=[act, state, state],
        out_shape=[jax.ShapeDtypeStruct(u.shape, F32), state_shape, state_shape],
        scratch_shapes=[pltpu.VMEM((gpb, n_rows, lc), F32)] * 2 + [pltpu.VMEM((n_rows, state_w), F32)] * 4,
        compiler_params=_params(("parallel", "parallel")),
        name="ssm",
    )(u, m, w_re, w_im, v_re, v_im, a_blk, h0re, h0im)

    def from_blocks(x):
        return jnp.transpose(x, (0, 2, 1, 3)).reshape(batch, N_GROUPS, SSM_STATE)

    return y, jnp.stack([from_blocks(hre), from_blocks(him)], axis=-1)


def _post_kernel(x_ref, o_ref, y_ref, u_ref, ga_ref, gs_ref, p_ref,
                 d_ref, wao_ref, wgv_ref, wgg_ref, wout_ref, gmlp_ref, wup_ref, wdown_ref,
                 gple_ref, wple_ref, wpg_ref, out_ref, *, ff_chunk):
    def rms(x, g_ref):
        ms = jnp.mean(x * x, axis=-1, keepdims=True)
        return (x * lax.rsqrt(ms + RMS_EPS) * g_ref[...]).astype(BF16)

    y_act = jax.nn.gelu(y_ref[...] + d_ref[...] * u_ref[...], approximate=True).astype(BF16)
    attn_out = _dot(o_ref[...].astype(BF16), wao_ref[...])
    ssm_out = _dot(y_act, wgv_ref[...]) * jax.nn.sigmoid(_dot(y_act, wgg_ref[...]))
    merged = jax.nn.sigmoid(ga_ref[...]) * attn_out + jax.nn.sigmoid(gs_ref[...]) * ssm_out
    x = x_ref[...] + _dot(merged.astype(BF16), wout_ref[...])
    h2 = rms(x, gmlp_ref)
    d_ff = wup_ref.shape[1]
    mlp = None
    for c0 in range(0, d_ff, ff_chunk):
        hid = jnp.square(jnp.maximum(_dot(h2, wup_ref[:, c0:c0 + ff_chunk]), 0.0)).astype(BF16)
        part = _dot(hid, wdown_ref[c0:c0 + ff_chunk, :])
        mlp = part if mlp is None else mlp + part
    x = x + mlp
    h3 = rms(x, gple_ref)
    out_ref[...] = x + _dot(p_ref[...].astype(BF16), wple_ref[...]) * jax.nn.sigmoid(_dot(h3, wpg_ref[...]))


def _post(x, o, y, u, ga, gs, p, d_skip, w):
    m, d = x.shape
    tm = min(ROW_TILE, m)
    row = lambda a: pl.BlockSpec((tm, a.shape[1]), lambda i: (i, 0))
    vec = lambda a: a.reshape(1, -1)
    acts = (x, o, y, u, ga, gs, p)
    consts = (vec(d_skip), w["w_attn_out"], w["w_glu_val"], w["w_glu_gate"], w["w_out"], vec(w["g_mlp"]),
              w["w_up"], w["w_down"], vec(w["g_ple"]), w["w_ple"], w["w_ple_gate"])
    return pl.pallas_call(
        functools.partial(_post_kernel, ff_chunk=1024),
        grid=(m // tm,),
        in_specs=[row(a) for a in acts] + [_resident(c.shape) for c in consts],
        out_specs=pl.BlockSpec((tm, d), lambda i: (i, 0)),
        out_shape=jax.ShapeDtypeStruct((m, d), F32),
        compiler_params=_params(("parallel",)),
        name="post",
    )(*acts, *consts)


def _layer(x, p_l, attend, h0, batch, seq, chunk, w):
    m = batch * seq
    q, k, v, u, ga, gs = _in_proj(x.reshape(m, -1), w["g_mix"], w["w_in"], w["bd"], w["g_q"], w["g_k"])
    o = attend(q, k, v)
    y, h_last = _ssm(u, h0, w["ssm_mats"][chunk], batch, seq, chunk)
    x_out = _post(x.reshape(m, -1), o, y, u, ga, gs, p_l.reshape(m, -1), w["ssm_d"], w)
    shape4 = (batch, seq, N_HEADS, HEAD_DIM)
    return x_out.reshape(x.shape), k.reshape(shape4), v.reshape(shape4), h_last


def kernel(x_prompt, x_sample, p_prompt, p_sample, cache_k, cache_v, state_ssm, page_table, g_mix, w_in, g_q, g_k, sb_bias, w_attn_out, ssm_a_re, ssm_a_im, ssm_log_dt, ssm_b_re, ssm_b_im, ssm_c_re, ssm_c_im, ssm_d, w_glu_val, w_glu_gate, w_out, g_mlp, w_up, w_down, g_ple, w_ple, w_ple_gate):
    depth = w_in.shape[0]
    batch, seq, _ = x_prompt.shape
    dec_batch, dec_seq, _ = x_sample.shape
    ck = cache_k.reshape(cache_k.shape[:2] + (-1, HEAD_DIM))
    cv = cache_v.reshape(cache_v.shape[:2] + (-1, HEAD_DIM))
    hd = jnp.arange(ATT_WIDTH, dtype=jnp.int32) // HEAD_DIM
    bd = jnp.where(hd[:, None] == hd[None, :], 1.0 / HEAD_DIM, 0.0).astype(BF16)

    xp, xs = x_prompt, x_sample
    outs = [[] for _ in range(6)]
    for i in range(depth):
        ssm_p = (ssm_a_re[i], ssm_a_im[i], ssm_log_dt[i], ssm_b_re[i], ssm_b_im[i], ssm_c_re[i], ssm_c_im[i])
        w = {
            "g_mix": g_mix[i], "w_in": w_in[i].astype(BF16), "bd": bd,
            "g_q": jnp.tile(g_q[i], N_HEADS).reshape(1, ATT_WIDTH),
            "g_k": jnp.tile(g_k[i], N_HEADS).reshape(1, ATT_WIDTH),
            "w_attn_out": w_attn_out[i].astype(BF16), "w_glu_val": w_glu_val[i].astype(BF16),
            "w_glu_gate": w_glu_gate[i].astype(BF16), "w_out": w_out[i].astype(BF16),
            "g_mlp": g_mlp[i], "w_up": w_up[i].astype(BF16), "w_down": w_down[i].astype(BF16),
            "g_ple": g_ple[i], "w_ple": w_ple[i].astype(BF16), "w_ple_gate": w_ple_gate[i].astype(BF16),
            "ssm_d": ssm_d[i],
            "ssm_mats": {L: _ssm_mats(*ssm_p, L) for L in sorted({PROMPT_CHUNK, dec_seq})},
        }
        bias = sb_bias[i].astype(F32)
        bias_rows = jnp.repeat(bias, dec_seq).reshape(N_HEADS * dec_seq, 1)

        attend_p = lambda q, k, v: _attn_prompt(q, k, v, bias, batch, seq)
        xp, kp, vp, hp = _layer(xp, p_prompt[i], attend_p, None, batch, seq, PROMPT_CHUNK, w)
        attend_s = lambda q, k, v: _attn_sample(q, k, v, ck, cv, i, page_table, bias_rows, dec_seq)
        xs, ks, vs, hs = _layer(xs, p_sample[i], attend_s, state_ssm[i], dec_batch, dec_seq, dec_seq, w)
        for lst, val in zip(outs, (kp, vp, hp, ks, vs, hs)):
            lst.append(val)
    k_p, v_p, s_p, k_s, v_s, s_s = (jnp.stack(o) for o in outs)
    return xp, xs, k_p, v_p, s_p, k_s, v_s, s_s
```

```python
import functools

import jax
import jax.numpy as jnp
from jax import lax
from jax.experimental import pallas as pl
from jax.experimental.pallas import tpu as pltpu

F32 = jnp.float32
BF16 = jnp.bfloat16

N_HEADS = 8
HEAD_DIM = 64
ATT_WIDTH = N_HEADS * HEAD_DIM
GROUP_CH = 16
N_GROUPS = 32
SSM_STATE = 64
D_SSM = N_GROUPS * GROUP_CH
RMS_EPS = 1e-6
LOG2E = 1.4426950408889634
LANES = 128
GROUPS_PER_BLOCK = LANES // GROUP_CH
VMEM_LIMIT = 56 * 1024 * 1024

PROMPT_CHUNK = 16
ATT_TK = 256
ATT_TQ = 2 * ATT_TK
ROW_TILE = 256


def _dot(a, b):
    return jnp.dot(a, b, preferred_element_type=F32)


def _dot_nt(a, b):
    return lax.dot_general(a, b, (((1,), (1,)), ((), ())), preferred_element_type=F32)


def _split(x):
    hi = x.astype(BF16)
    lo = (x - hi.astype(F32)).astype(BF16)
    return hi, lo


def _dot3(a, b):
    ah, al = _split(a)
    bh, bl = _split(b)
    return _dot(ah, bh) + _dot(al, bh) + _dot(ah, bl)


def _params(sem):
    return pltpu.CompilerParams(dimension_semantics=sem, vmem_limit_bytes=VMEM_LIMIT)


def _resident(shape):
    nd = len(shape)
    return pl.BlockSpec(shape, lambda *_: (0,) * nd, pipeline_mode=pl.Buffered(1))


def _in_proj_kernel(x_ref, gmix_ref, w_ref, bd_ref, gq_ref, gk_ref,
                    q_ref, k_ref, v_ref, u_ref, ga_ref, gs_ref):
    x = x_ref[...]
    ms = jnp.mean(x * x, axis=-1, keepdims=True)
    h = (x * lax.rsqrt(ms + RMS_EPS) * gmix_ref[...]).astype(BF16)

    def head_norm(p, g_ref):
        hi, lo = _split(p * p)
        msh = _dot(hi, bd_ref[...]) + _dot(lo, bd_ref[...])
        return p * lax.rsqrt(msh + RMS_EPS) * g_ref[...]

    a = ATT_WIDTH
    q_ref[...] = head_norm(_dot(h, w_ref[:, 0:a]), gq_ref)
    k_ref[...] = head_norm(_dot(h, w_ref[:, a:2 * a]), gk_ref)
    v_ref[...] = _dot(h, w_ref[:, 2 * a:3 * a])
    u_ref[...] = _dot(h, w_ref[:, 3 * a:3 * a + D_SSM])
    o = 3 * a + D_SSM
    d = ga_ref.shape[-1]
    ga_ref[...] = _dot(h, w_ref[:, o:o + d])
    gs_ref[...] = _dot(h, w_ref[:, o + d:o + 2 * d])


def _in_proj(x, g_mix, w_in_bf, bd, gq_t, gk_t):
    m, d = x.shape
    tm = min(ROW_TILE, m)
    row = lambda w: pl.BlockSpec((tm, w), lambda i: (i, 0))
    widths = (ATT_WIDTH, ATT_WIDTH, ATT_WIDTH, D_SSM, d, d)
    return pl.pallas_call(
        _in_proj_kernel,
        grid=(m // tm,),
        in_specs=[row(d), _resident((1, d)), _resident(w_in_bf.shape), _resident(bd.shape),
                  _resident((1, ATT_WIDTH)), _resident((1, ATT_WIDTH))],
        out_specs=[row(w) for w in widths],
        out_shape=[jax.ShapeDtypeStruct((m, w), F32) for w in widths],
        compiler_params=_params(("parallel",)),
        name="in_proj",
    )(x, g_mix.reshape(1, d), w_in_bf, bd, gq_t, gk_t)


def _sb_block(nz, mask, tri, carry):
    m = jnp.minimum(nz, 0.0)
    e = jnp.exp2(m + (m - nz))
    ls = m - jnp.log(1.0 + e) * LOG2E
    if mask is not None:
        ls = jnp.where(mask, ls, 0.0)
    if tri.shape[0] == tri.shape[1]:
        parts = ls.astype(BF16)
    else:
        parts = jnp.concatenate(_split(ls), axis=1)
    later = _dot(parts, tri)
    w = jnp.exp2((ls - nz) + (later + carry))
    if mask is not None:
        w = jnp.where(mask, w, 0.0)
    return w, carry + (later[:, 0:1] + ls[:, 0:1])


def _later_keys(n, copies):
    assert copies in (1, 2)
    r = lax.broadcasted_iota(jnp.int32, (copies * n, n), 0)
    c = lax.broadcasted_iota(jnp.int32, (copies * n, n), 1)
    return jnp.where(jnp.where(r >= n, r - n, r) > c, 1.0, 0.0).astype(BF16)


def _attn_prompt_kernel(bias_ref, q_ref, k_ref, v_ref, o_ref, kb_s, vb_s, acc_ref, carry_ref,
                        *, tq, tk, seq):
    hp = pl.program_id(1)
    qi = pl.program_id(2)
    lane = lax.broadcasted_iota(jnp.int32, (1, LANES), 1)
    first_head = lane < HEAD_DIM

    @pl.when(qi == 0)
    def _stage_keys_values():
        def stage(i, _):
            rows = pl.ds(pl.multiple_of(i * tk, tk), tk)
            kb_s[rows, :] = k_ref[rows, :].astype(BF16)
            vb_s[rows, :] = v_ref[rows, :].astype(BF16)
            return 0
        lax.fori_loop(0, seq // tk, stage, 0)

    qs = q_ref[...] * (-(HEAD_DIM ** -0.5) * LOG2E)
    q2 = jnp.concatenate([jnp.where(first_head, qs, 0.0), jnp.where(first_head, 0.0, qs)], axis=0).astype(BF16)
    nbias = jnp.concatenate([jnp.full((tq, 1), bias_ref[2 * hp + h] * (-LOG2E), F32) for h in range(2)], axis=0)
    tri = _later_keys(tk, 1)
    r = lax.broadcasted_iota(jnp.int32, (2 * tq, tk), 0)
    c = lax.broadcasted_iota(jnp.int32, (2 * tq, tk), 1)
    q_pos = jnp.where(r < tq, r, r - tq) + qi * tq

    def block(kb, mask, carry):
        rows = pl.ds(pl.multiple_of(kb * tk, tk), tk)
        nz = _dot_nt(q2, kb_s[rows, :]) + nbias
        w, carry = _sb_block(nz, mask, tri, carry)
        pv = _dot(w.astype(BF16), vb_s[rows, :])
        return jnp.where(first_head, pv[:tq], pv[tq:]), carry

    def run(blocks, masked, carry):
        total = None
        for kb in blocks:
            contrib, carry = block(kb, (c + kb * tk < q_pos) if masked else None, carry)
            total = contrib if total is None else total + contrib
        return total, carry

    per_q = tq // tk
    top = per_q * (qi + 1) - 1
    total, carry = run([top - d for d in range(per_q)], True, jnp.zeros((2 * tq, 1), F32))
    acc_ref[...] = total
    carry_ref[...] = carry

    def below_diagonal(i, _):
        first = top - per_q * (i + 1)
        total, carry = run([first - d for d in range(per_q)], False, carry_ref[...])
        acc_ref[...] += total
        carry_ref[...] = carry
        return 0
    lax.fori_loop(0, qi, below_diagonal, 0)
    o_ref[...] = acc_ref[...]


def _attn_prompt(q, k, v, bias, batch, seq):
    tq, tk = ATT_TQ, ATT_TK
    nq = seq // tq
    kern = functools.partial(_attn_prompt_kernel, tq=tq, tk=tk, seq=seq)
    return pl.pallas_call(
        kern,
        grid=(batch, N_HEADS // 2, nq),
        in_specs=[pl.BlockSpec(memory_space=pltpu.SMEM),
                  pl.BlockSpec((tq, LANES), lambda b, hp, i: (b * nq + i, hp)),
                  pl.BlockSpec((seq, LANES), lambda b, hp, i: (b, hp)),
                  pl.BlockSpec((seq, LANES), lambda b, hp, i: (b, hp))],
        out_specs=pl.BlockSpec((tq, LANES), lambda b, hp, i: (b * nq + i, hp)),
        out_shape=jax.ShapeDtypeStruct(q.shape, F32),
        scratch_shapes=[pltpu.VMEM((seq, LANES), BF16)] * 2
                       + [pltpu.VMEM((tq, LANES), F32), pltpu.VMEM((2 * tq, 1), F32)],
        compiler_params=_params(("parallel", "parallel", "arbitrary")),
        name="attn_prompt",
    )(bias, q, k, v)


def _attn_sample_kernel(pt_ref, bias_ref, hmask_ref, nmask_ref, q_ref, kn_ref, vn_ref, *rest,
                        n_pages, t_new, page):
    del pt_ref
    k_pages = rest[:n_pages]
    v_pages = rest[n_pages:2 * n_pages]
    o_ref = rest[2 * n_pages]
    rows = N_HEADS * t_new
    same_head = hmask_ref[...] > 0.5
    q = q_ref[...] * (-(HEAD_DIM ** -0.5) * LOG2E)
    q_bd = jnp.where(same_head, jnp.concatenate([q] * N_HEADS, axis=0), 0.0).astype(BF16)
    nbias = bias_ref[...] * (-LOG2E)

    pad = jnp.zeros((page - t_new, ATT_WIDTH), F32)
    k_new = jnp.concatenate([kn_ref[...], pad], axis=0).astype(BF16)
    v_new = jnp.concatenate([vn_ref[...], pad], axis=0).astype(BF16)
    new_mask = nmask_ref[...] > 0.5
    nz = _dot_nt(q_bd, k_new) + nbias
    w, carry = _sb_block(nz, new_mask, _later_keys(page, 2), jnp.zeros((rows, 1), F32))
    acc = _dot(w.astype(BF16), v_new)

    per_block = 2 if n_pages % 2 == 0 else 1
    tri = _later_keys(per_block * page, 2)

    def key_block(page_refs, jb):
        pages = [page_refs[jb * per_block + i][...] for i in range(per_block)]
        return (pages[0] if per_block == 1 else jnp.concatenate(pages, axis=1)).astype(BF16)

    for jb in reversed(range(n_pages // per_block)):
        nz = _dot(q_bd, key_block(k_pages, jb)) + nbias
        w, carry = _sb_block(nz, None, tri, carry)
        acc = acc + _dot_nt(w.astype(BF16), key_block(v_pages, jb))
    acc = jnp.where(same_head, acc, 0.0)
    out = acc[0:t_new, :]
    for h in range(1, N_HEADS):
        out = out + acc[h * t_new:(h + 1) * t_new, :]
    o_ref[...] = out


def _attn_sample(q, k_new, v_new, cache_k, cache_v, layer, page_table, bias_rows, t_new):
    n_seq, n_pages = page_table.shape
    page = cache_k.shape[3]
    page_spec = lambda j: pl.BlockSpec((None, None, ATT_WIDTH, page),
                                       lambda b, pt, j=j: (layer, pt[b, j], 0, 0))
    tok = pl.BlockSpec((t_new, ATT_WIDTH), lambda b, pt: (b, 0))
    rows = N_HEADS * t_new
    whole = lambda w: pl.BlockSpec((rows, w), lambda b, pt: (0, 0))
    row_head = jnp.arange(rows, dtype=jnp.int32)[:, None] // t_new
    row_query = jnp.arange(rows, dtype=jnp.int32)[:, None] % t_new
    head_mask = (row_head == jnp.arange(ATT_WIDTH, dtype=jnp.int32)[None, :] // HEAD_DIM).astype(F32)
    new_mask = (jnp.arange(page, dtype=jnp.int32)[None, :] < row_query).astype(F32)
    kern = functools.partial(_attn_sample_kernel, n_pages=n_pages, t_new=t_new, page=page)
    grid_spec = pltpu.PrefetchScalarGridSpec(
        num_scalar_prefetch=1,
        grid=(n_seq,),
        in_specs=[whole(1), whole(ATT_WIDTH), whole(page), tok, tok, tok]
                 + [page_spec(j) for j in range(n_pages)] * 2,
        out_specs=tok,
    )
    return pl.pallas_call(
        kern,
        grid_spec=grid_spec,
        out_shape=jax.ShapeDtypeStruct(q.shape, F32),
        compiler_params=_params(("arbitrary",)),
        name="attn_sample",
    )(page_table, bias_rows, head_mask, new_mask, q, k_new, v_new,
      *([cache_k] * n_pages), *([cache_v] * n_pages))


def _ssm_kernel(u_ref, m_ref, wre_ref, wim_ref, vre_ref, vim_ref, a_ref, h0re_ref, h0im_ref,
                y_ref, hre_ref, him_ref, ug_s, yg_s, zre_s, zim_s, inre_s, inim_s, *, L, n_steps, bp):
    n_rows = n_steps * bp
    p = SSM_STATE
    lane = lax.broadcasted_iota(jnp.int32, (1, LANES), 1)
    in_slot = [(lane >= j * GROUP_CH) & (lane < (j + 1) * GROUP_CH) for j in range(GROUPS_PER_BLOCK)]

    def regroup(pieces, dst):
        out = None
        for j, piece in enumerate(pieces):
            shift = ((j - dst) * GROUP_CH) % LANES
            moved = pltpu.roll(piece, shift, 1) if shift else piece
            out = moved if out is None else jnp.where(in_slot[j], moved, out)
        return out

    xs = [u_ref[pl.ds(s, n_rows, stride=L), :] for s in range(L)]
    n_half = L // GROUPS_PER_BLOCK
    for g in range(GROUPS_PER_BLOCK):
        halves = [regroup(xs[h * GROUPS_PER_BLOCK:(h + 1) * GROUPS_PER_BLOCK], g) for h in range(n_half)]
        u_g = halves[0] if n_half == 1 else jnp.concatenate(halves, axis=1)
        ug_s[g] = u_g
        zre_s[:, g * p:(g + 1) * p] = _dot3(u_g, wre_ref[g])
        zim_s[:, g * p:(g + 1) * p] = _dot3(u_g, wim_ref[g])

    ar = a_ref[0:1, :]
    ai = a_ref[1:2, :]

    def step(k, carry):
        hr, hi = carry
        rows = pl.ds(k * bp, bp)
        inre_s[rows, :] = hr
        inim_s[rows, :] = hi
        zr = zre_s[rows, :]
        zi = zim_s[rows, :]
        return ar * hr - ai * hi + zr, ar * hi + ai * hr + zi

    hr, hi = lax.fori_loop(0, n_steps, step, (h0re_ref[...], h0im_ref[...]))
    hre_ref[...] = hr
    him_ref[...] = hi

    for g in range(GROUPS_PER_BLOCK):
        yg_s[g] = (_dot3(ug_s[g], m_ref[g]) + _dot3(inre_s[:, g * p:(g + 1) * p], vre_ref[g])
                   + _dot3(inim_s[:, g * p:(g + 1) * p], vim_ref[g]))
    for t in range(L):
        h, j = divmod(t, GROUPS_PER_BLOCK)
        pieces = [yg_s[g, :, h * LANES:(h + 1) * LANES] for g in range(GROUPS_PER_BLOCK)]
        y_ref[pl.ds(t, n_rows, stride=L), :] = regroup(pieces, j)


def _ssm_mats(a_re, a_im, log_dt, b_re, b_im, c_re, c_im, L):
    hp = lax.Precision.HIGHEST
    dt = jnp.exp(log_dt)[:, None]
    mag = jnp.exp(a_re * dt)
    ab_re, ab_im = mag * jnp.cos(a_im * dt), mag * jnp.sin(a_im * dt)
    den = a_re * a_re + a_im * a_im
    nr, ni = ab_re - 1.0, ab_im
    f_re = (nr * a_re + ni * a_im) / den
    f_im = (ni * a_re - nr * a_im) / den
    bb_re = f_re[..., None] * b_re - f_im[..., None] * b_im
    bb_im = f_re[..., None] * b_im + f_im[..., None] * b_re
    pr, pi = [jnp.ones_like(ab_re)], [jnp.zeros_like(ab_re)]
    for _ in range(L):
        pr, pi = pr + [ab_re * pr[-1] - ab_im * pi[-1]], pi + [ab_re * pi[-1] + ab_im * pr[-1]]
    p_re, p_im = jnp.stack(pr), jnp.stack(pi)
    abr = p_re[:L, :, :, None] * bb_re - p_im[:L, :, :, None] * bb_im
    abi = p_re[:L, :, :, None] * bb_im + p_im[:L, :, :, None] * bb_re
    kern = (jnp.einsum('gcp,tgpd->tgcd', c_re, abr, precision=hp)
            - jnp.einsum('gcp,tgpd->tgcd', c_im, abi, precision=hp))
    s = jnp.arange(L)[:, None]
    t = jnp.arange(L)[None, :]
    tau = t - s
    m = jnp.where((tau >= 0)[:, :, None, None, None], kern[jnp.clip(tau, 0, L - 1)], 0.0)
    m = jnp.transpose(m, (2, 0, 4, 1, 3)).reshape(N_GROUPS, L * GROUP_CH, L * GROUP_CH)
    lc = L * GROUP_CH
    w_re = jnp.transpose(abr[::-1], (1, 0, 3, 2)).reshape(N_GROUPS, lc, SSM_STATE)
    w_im = jnp.transpose(abi[::-1], (1, 0, 3, 2)).reshape(N_GROUPS, lc, SSM_STATE)
    cr = jnp.transpose(c_re, (0, 2, 1))[:, :, None, :]
    ci = jnp.transpose(c_im, (0, 2, 1))[:, :, None, :]
    qr = jnp.transpose(p_re[1:], (1, 2, 0))[..., None]
    qi = jnp.transpose(p_im[1:], (1, 2, 0))[..., None]
    v_re = (cr * qr - ci * qi).reshape(N_GROUPS, SSM_STATE, lc)
    v_im = (-cr * qi - ci * qr).reshape(N_GROUPS, SSM_STATE, lc)
    a_l = jnp.stack([p_re[L], p_im[L]], axis=1)
    return m, w_re, w_im, v_re, v_im, a_l


def _ssm(u, h0, mats, batch, seq, L):
    m, w_re, w_im, v_re, v_im, a_l = mats
    gpb = GROUPS_PER_BLOCK
    n_cb = N_GROUPS // gpb
    lc = L * GROUP_CH
    state_w = gpb * SSM_STATE
    if seq == L:
        n_bblk, bp, n_steps = 1, batch, 1
    else:
        n_bblk, bp, n_steps = batch, 1, seq // L
    tok = bp * n_steps * L
    n_rows = bp * n_steps

    def to_blocks(x):
        x = x.reshape(n_bblk, bp, n_cb, state_w)
        return jnp.transpose(x, (0, 2, 1, 3))

    if h0 is None:
        h0re = h0im = jnp.zeros((n_bblk, n_cb, bp, state_w), F32)
    else:
        h0re, h0im = to_blocks(h0[..., 0]), to_blocks(h0[..., 1])
    a_blk = jnp.transpose(a_l.reshape(n_cb, gpb, 2, SSM_STATE), (0, 2, 1, 3)).reshape(n_cb, 2, state_w)

    grp = lambda a, b: pl.BlockSpec((gpb, a, b), lambda i, cb: (cb, 0, 0))
    act = pl.BlockSpec((tok, LANES), lambda i, cb: (i, cb))
    state = pl.BlockSpec((None, None, bp, state_w), lambda i, cb: (i, cb, 0, 0))
    state_shape = jax.ShapeDtypeStruct((n_bblk, n_cb, bp, state_w), F32)
    kern = functools.partial(_ssm_kernel, L=L, n_steps=n_steps, bp=bp)
    y, hre, him = pl.pallas_call(
        kern,
        grid=(n_bblk, n_cb),
        in_specs=[act, grp(lc, lc), grp(lc, SSM_STATE), grp(lc, SSM_STATE),
                  grp(SSM_STATE, lc), grp(SSM_STATE, lc),
                  pl.BlockSpec((None, 2, state_w), lambda i, cb: (cb, 0, 0)), state, state],
        out_specs=[act, state, state],
        out_shape=[jax.ShapeDtypeStruct(u.shape, F32), state_shape, state_shape],
        scratch_shapes=[pltpu.VMEM((gpb, n_rows, lc), F32)] * 2 + [pltpu.VMEM((n_rows, state_w), F32)] * 4,
        compiler_params=_params(("parallel", "parallel")),
        name="ssm",
    )(u, m, w_re, w_im, v_re, v_im, a_blk, h0re, h0im)

    def from_blocks(x):
        return jnp.transpose(x, (0, 2, 1, 3)).reshape(batch, N_GROUPS, SSM_STATE)

    return y, jnp.stack([from_blocks(hre), from_blocks(him)], axis=-1)


def _post_kernel(x_ref, o_ref, y_ref, u_ref, ga_ref, gs_ref, p_ref,
                 d_ref, wao_ref, wgv_ref, wgg_ref, wout_ref, gmlp_ref, wup_ref, wdown_ref,
                 gple_ref, wple_ref, wpg_ref, out_ref, *, ff_chunk):
    def rms(x, g_ref):
        ms = jnp.mean(x * x, axis=-1, keepdims=True)
        return (x * lax.rsqrt(ms + RMS_EPS) * g_ref[...]).astype(BF16)

    y_act = jax.nn.gelu(y_ref[...] + d_ref[...] * u_ref[...], approximate=True).astype(BF16)
    attn_out = _dot(o_ref[...].astype(BF16), wao_ref[...])
    ssm_out = _dot(y_act, wgv_ref[...]) * jax.nn.sigmoid(_dot(y_act, wgg_ref[...]))
    merged = jax.nn.sigmoid(ga_ref[...]) * attn_out + jax.nn.sigmoid(gs_ref[...]) * ssm_out
    x = x_ref[...] + _dot(merged.astype(BF16), wout_ref[...])
    h2 = rms(x, gmlp_ref)
    d_ff = wup_ref.shape[1]
    mlp = None
    for c0 in range(0, d_ff, ff_chunk):
        hid = jnp.square(jnp.maximum(_dot(h2, wup_ref[:, c0:c0 + ff_chunk]), 0.0)).astype(BF16)
        part = _dot(hid, wdown_ref[c0:c0 + ff_chunk, :])
        mlp = part if mlp is None else mlp + part
    x = x + mlp
    h3 = rms(x, gple_ref)
    out_ref[...] = x + _dot(p_ref[...].astype(BF16), wple_ref[...]) * jax.nn.sigmoid(_dot(h3, wpg_ref[...]))


def _post(x, o, y, u, ga, gs, p, d_skip, w):
    m, d = x.shape
    tm = min(ROW_TILE, m)
    row = lambda a: pl.BlockSpec((tm, a.shape[1]), lambda i: (i, 0))
    vec = lambda a: a.reshape(1, -1)
    acts = (x, o, y, u, ga, gs, p)
    consts = (vec(d_skip), w["w_attn_out"], w["w_glu_val"], w["w_glu_gate"], w["w_out"], vec(w["g_mlp"]),
              w["w_up"], w["w_down"], vec(w["g_ple"]), w["w_ple"], w["w_ple_gate"])
    return pl.pallas_call(
        functools.partial(_post_kernel, ff_chunk=1024),
        grid=(m // tm,),
        in_specs=[row(a) for a in acts] + [_resident(c.shape) for c in consts],
        out_specs=pl.BlockSpec((tm, d), lambda i: (i, 0)),
        out_shape=jax.ShapeDtypeStruct((m, d), F32),
        compiler_params=_params(("parallel",)),
        name="post",
    )(*acts, *consts)


def _layer(x, p_l, attend, h0, batch, seq, chunk, w):
    m = batch * seq
    q, k, v, u, ga, gs = _in_proj(x.reshape(m, -1), w["g_mix"], w["w_in"], w["bd"], w["g_q"], w["g_k"])
    o = attend(q, k, v)
    y, h_last = _ssm(u, h0, w["ssm_mats"][chunk], batch, seq, chunk)
    x_out = _post(x.reshape(m, -1), o, y, u, ga, gs, p_l.reshape(m, -1), w["ssm_d"], w)
    shape4 = (batch, seq, N_HEADS, HEAD_DIM)
    return x_out.reshape(x.shape), k.reshape(shape4), v.reshape(shape4), h_last


def kernel(x_prompt, x_sample, p_prompt, p_sample, cache_k, cache_v, state_ssm, page_table, g_mix, w_in, g_q, g_k, sb_bias, w_attn_out, ssm_a_re, ssm_a_im, ssm_log_dt, ssm_b_re, ssm_b_im, ssm_c_re, ssm_c_im, ssm_d, w_glu_val, w_glu_gate, w_out, g_mlp, w_up, w_down, g_ple, w_ple, w_ple_gate):
    depth = w_in.shape[0]
    batch, seq, _ = x_prompt.shape
    dec_batch, dec_seq, _ = x_sample.shape
    to_pages = lambda c: jnp.transpose(c, (0, 1, 3, 4, 2)).reshape(c.shape[:2] + (ATT_WIDTH, c.shape[2]))
    ck, cv = to_pages(cache_k), to_pages(cache_v)
    hd = jnp.arange(ATT_WIDTH, dtype=jnp.int32) // HEAD_DIM
    bd = jnp.where(hd[:, None] == hd[None, :], 1.0 / HEAD_DIM, 0.0).astype(BF16)

    xp, xs = x_prompt, x_sample
    outs = [[] for _ in range(6)]
    for i in range(depth):
        ssm_p = (ssm_a_re[i], ssm_a_im[i], ssm_log_dt[i], ssm_b_re[i], ssm_b_im[i], ssm_c_re[i], ssm_c_im[i])
        w = {
            "g_mix": g_mix[i], "w_in": w_in[i].astype(BF16), "bd": bd,
            "g_q": jnp.tile(g_q[i], N_HEADS).reshape(1, ATT_WIDTH),
            "g_k": jnp.tile(g_k[i], N_HEADS).reshape(1, ATT_WIDTH),
            "w_attn_out": w_attn_out[i].astype(BF16), "w_glu_val": w_glu_val[i].astype(BF16),
            "w_glu_gate": w_glu_gate[i].astype(BF16), "w_out": w_out[i].astype(BF16),
            "g_mlp": g_mlp[i], "w_up": w_up[i].astype(BF16), "w_down": w_down[i].astype(BF16),
            "g_ple": g_ple[i], "w_ple": w_ple[i].astype(BF16), "w_ple_gate": w_ple_gate[i].astype(BF16),
            "ssm_d": ssm_d[i],
            "ssm_mats": {L: _ssm_mats(*ssm_p, L) for L in sorted({PROMPT_CHUNK, dec_seq})},
        }
        bias = sb_bias[i].astype(F32)
        bias_rows = jnp.repeat(bias, dec_seq).reshape(N_HEADS * dec_seq, 1)

        attend_p = lambda q, k, v: _attn_prompt(q, k, v, bias, batch, seq)
        xp, kp, vp, hp = _layer(xp, p_prompt[i], attend_p, None, batch, seq, PROMPT_CHUNK, w)
        attend_s = lambda q, k, v: _attn_sample(q, k, v, ck, cv, i, page_table, bias_rows, dec_seq)
        xs, ks, vs, hs = _layer(xs, p_sample[i], attend_s, state_ssm[i], dec_batch, dec_seq, dec_seq, w)
        for lst, val in zip(outs, (kp, vp, hp, ks, vs, hs)):
            lst.append(val)
    k_p, v_p, s_p, k_s, v_s, s_s = (jnp.stack(o) for o in outs)
    return xp, xs, k_p, v_p, s_p, k_s, v_s, s_s
```

```python
import functools

import jax
import jax.numpy as jnp
from jax import lax
from jax.experimental import pallas as pl
from jax.experimental.pallas import tpu as pltpu

F32 = jnp.float32
BF16 = jnp.bfloat16

N_HEADS = 8
HEAD_DIM = 64
ATT_WIDTH = N_HEADS * HEAD_DIM
GROUP_CH = 16
N_GROUPS = 32
SSM_STATE = 64
D_SSM = N_GROUPS * GROUP_CH
RMS_EPS = 1e-6
LOG2E = 1.4426950408889634
Q_SCALE = -(HEAD_DIM ** -0.5) * LOG2E
LANES = 128
GROUPS_PER_BLOCK = LANES // GROUP_CH
VMEM_LIMIT = 56 * 1024 * 1024

PROMPT_CHUNK = 16
SSM_ROW_CHUNK = 64
ATT_TK = 256
ATT_TQ = 2 * ATT_TK
ROW_TILE = 256


def _dot(a, b):
    return jnp.dot(a, b, preferred_element_type=F32)


def _dot_nt(a, b):
    return lax.dot_general(a, b, (((1,), (1,)), ((), ())), preferred_element_type=F32)


def _split(x):
    hi = x.astype(BF16)
    lo = (x - hi.astype(F32)).astype(BF16)
    return hi, lo


def _dot3(a, b):
    ah, al = _split(a)
    bh, bl = _split(b)
    return _dot(ah, bh) + _dot(al, bh) + _dot(ah, bl)


def _params(sem):
    return pltpu.CompilerParams(dimension_semantics=sem, vmem_limit_bytes=VMEM_LIMIT)


def _resident(shape):
    nd = len(shape)
    return pl.BlockSpec(shape, lambda *_: (0,) * nd, pipeline_mode=pl.Buffered(1))


def _in_proj_kernel(x_ref, gmix_ref, w_ref, bd_ref, gq_ref, gk_ref,
                    q_ref, k_ref, v_ref, u_ref, ga_ref, gs_ref):
    x = x_ref[...]
    ms = jnp.mean(x * x, axis=-1, keepdims=True)
    h = (x * lax.rsqrt(ms + RMS_EPS) * gmix_ref[...]).astype(BF16)

    def head_norm(p, g_ref):
        msh = _dot((p * p).astype(BF16), bd_ref[...])
        return p * lax.rsqrt(msh + RMS_EPS) * g_ref[...]

    a = ATT_WIDTH
    q_ref[...] = (head_norm(_dot(h, w_ref[:, 0:a]), gq_ref) * Q_SCALE).astype(BF16)
    k_ref[...] = head_norm(_dot(h, w_ref[:, a:2 * a]), gk_ref)
    v_ref[...] = _dot(h, w_ref[:, 2 * a:3 * a])
    u_ref[...] = _dot(h, w_ref[:, 3 * a:3 * a + D_SSM])
    o = 3 * a + D_SSM
    d = ga_ref.shape[-1]
    ga_ref[...] = jax.nn.sigmoid(_dot(h, w_ref[:, o:o + d])).astype(BF16)
    gs_ref[...] = jax.nn.sigmoid(_dot(h, w_ref[:, o + d:o + 2 * d])).astype(BF16)


def _in_proj(x, g_mix, w_in_bf, bd, gq_t, gk_t):
    m, d = x.shape
    tm = min(ROW_TILE, m)
    row = lambda w: pl.BlockSpec((tm, w), lambda i: (i, 0))
    widths = (ATT_WIDTH, ATT_WIDTH, ATT_WIDTH, D_SSM, d, d)
    dtypes = (BF16, F32, F32, F32, BF16, BF16)
    return pl.pallas_call(
        _in_proj_kernel,
        grid=(m // tm,),
        in_specs=[row(d), _resident((1, d)), _resident(w_in_bf.shape), _resident(bd.shape),
                  _resident((1, ATT_WIDTH)), _resident((1, ATT_WIDTH))],
        out_specs=[row(w) for w in widths],
        out_shape=[jax.ShapeDtypeStruct((m, w), t) for w, t in zip(widths, dtypes)],
        compiler_params=_params(("parallel",)),
        name="in_proj",
    )(x, g_mix.reshape(1, d), w_in_bf, bd, gq_t, gk_t)


def _sb_block(nz, mask, tri, carry):
    m = jnp.minimum(nz, 0.0)
    e = jnp.exp2(m + (m - nz))
    ls = m - jnp.log(1.0 + e) * LOG2E
    if mask is not None:
        ls = jnp.where(mask, ls, 0.0)
    if tri.shape[0] == tri.shape[1]:
        parts = ls.astype(BF16)
    else:
        parts = jnp.concatenate(_split(ls), axis=1)
    later = _dot(parts, tri)
    w = jnp.exp2((ls - nz) + (later + carry))
    if mask is not None:
        w = jnp.where(mask, w, 0.0)
    return w, carry + (later[:, 0:1] + ls[:, 0:1])


def _later_keys(n, copies):
    assert copies in (1, 2)
    r = lax.broadcasted_iota(jnp.int32, (copies * n, n), 0)
    c = lax.broadcasted_iota(jnp.int32, (copies * n, n), 1)
    return jnp.where(jnp.where(r >= n, r - n, r) > c, 1.0, 0.0).astype(BF16)


def _attn_prompt_kernel(bias_ref, q_ref, k_ref, v_ref, o_ref, kb_s, vb_s, acc_ref, carry_ref,
                        *, tq, tk, seq):
    hp = pl.program_id(1)
    qi = pl.program_id(2)
    lane = lax.broadcasted_iota(jnp.int32, (1, LANES), 1)
    first_head = lane < HEAD_DIM

    @pl.when(qi == 0)
    def _stage_keys_values():
        def stage(i, _):
            rows = pl.ds(pl.multiple_of(i * tk, tk), tk)
            kb_s[rows, :] = k_ref[rows, :].astype(BF16)
            vb_s[rows, :] = v_ref[rows, :].astype(BF16)
            return 0
        lax.fori_loop(0, seq // tk, stage, 0)

    qs = q_ref[...]
    q2 = jnp.concatenate([jnp.where(first_head, qs, 0), jnp.where(first_head, 0, qs)], axis=0)
    nbias = jnp.concatenate([jnp.full((tq, 1), bias_ref[2 * hp + h] * (-LOG2E), F32) for h in range(2)], axis=0)
    tri = _later_keys(tk, 1)
    r = lax.broadcasted_iota(jnp.int32, (2 * tq, tk), 0)
    c = lax.broadcasted_iota(jnp.int32, (2 * tq, tk), 1)
    q_pos = jnp.where(r < tq, r, r - tq) + qi * tq

    def block(kb, mask, carry):
        rows = pl.ds(pl.multiple_of(kb * tk, tk), tk)
        nz = _dot_nt(q2, kb_s[rows, :]) + nbias
        w, carry = _sb_block(nz, mask, tri, carry)
        pv = _dot(w.astype(BF16), vb_s[rows, :])
        return jnp.where(first_head, pv[:tq], pv[tq:]), carry

    def run(blocks, masked, carry):
        total = None
        for kb in blocks:
            contrib, carry = block(kb, (c + kb * tk < q_pos) if masked else None, carry)
            total = contrib if total is None else total + contrib
        return total, carry

    per_q = tq // tk
    top = per_q * (qi + 1) - 1
    total, carry = run([top - d for d in range(per_q)], True, jnp.zeros((2 * tq, 1), F32))
    acc_ref[...] = total
    carry_ref[...] = carry

    def below_diagonal(i, _):
        first = top - per_q * (i + 1)
        total, carry = run([first - d for d in range(per_q)], False, carry_ref[...])
        acc_ref[...] += total
        carry_ref[...] = carry
        return 0
    lax.fori_loop(0, qi, below_diagonal, 0)
    o_ref[...] = acc_ref[...].astype(BF16)


def _attn_prompt(q, k, v, bias, batch, seq):
    tq, tk = ATT_TQ, ATT_TK
    nq = seq // tq
    kern = functools.partial(_attn_prompt_kernel, tq=tq, tk=tk, seq=seq)
    return pl.pallas_call(
        kern,
        grid=(batch, N_HEADS // 2, nq),
        in_specs=[pl.BlockSpec(memory_space=pltpu.SMEM),
                  pl.BlockSpec((tq, LANES), lambda b, hp, i: (b * nq + i, hp)),
                  pl.BlockSpec((seq, LANES), lambda b, hp, i: (b, hp)),
                  pl.BlockSpec((seq, LANES), lambda b, hp, i: (b, hp))],
        out_specs=pl.BlockSpec((tq, LANES), lambda b, hp, i: (b * nq + i, hp)),
        out_shape=jax.ShapeDtypeStruct(q.shape, BF16),
        scratch_shapes=[pltpu.VMEM((seq, LANES), BF16)] * 2
                       + [pltpu.VMEM((tq, LANES), F32), pltpu.VMEM((2 * tq, 1), F32)],
        compiler_params=_params(("parallel", "parallel", "arbitrary")),
        name="attn_prompt",
    )(bias, q, k, v)


def _attn_sample_kernel(pt_ref, bias_ref, hmask_ref, nmask_ref, q_ref, kn_ref, vn_ref, *rest,
                        n_pages, t_new, page):
    del pt_ref
    k_pages = rest[:n_pages]
    v_pages = rest[n_pages:2 * n_pages]
    o_ref = rest[2 * n_pages]
    rows = N_HEADS * t_new
    same_head = hmask_ref[...] > 0.5
    q = q_ref[...].astype(F32)
    q_bd = jnp.where(same_head, jnp.concatenate([q] * N_HEADS, axis=0), 0.0).astype(BF16)
    nbias = bias_ref[...] * (-LOG2E)

    pad = jnp.zeros((page - t_new, ATT_WIDTH), F32)
    k_new = jnp.concatenate([kn_ref[...], pad], axis=0).astype(BF16)
    v_new = jnp.concatenate([vn_ref[...], pad], axis=0).astype(BF16)
    new_mask = nmask_ref[...] > 0.5
    nz = _dot_nt(q_bd, k_new) + nbias
    w, carry = _sb_block(nz, new_mask, _later_keys(page, 2), jnp.zeros((rows, 1), F32))
    acc = _dot(w.astype(BF16), v_new)

    per_block = 2 if n_pages % 2 == 0 else 1
    tri = _later_keys(per_block * page, 2)

    def key_block(page_refs, jb):
        pages = [page_refs[jb * per_block + i][...] for i in range(per_block)]
        return (pages[0] if per_block == 1 else jnp.concatenate(pages, axis=1)).astype(BF16)

    for jb in reversed(range(n_pages // per_block)):
        nz = _dot(q_bd, key_block(k_pages, jb)) + nbias
        w, carry = _sb_block(nz, None, tri, carry)
        acc = acc + _dot_nt(w.astype(BF16), key_block(v_pages, jb))
    acc = jnp.where(same_head, acc, 0.0)
    out = acc[0:t_new, :]
    for h in range(1, N_HEADS):
        out = out + acc[h * t_new:(h + 1) * t_new, :]
    o_ref[...] = out.astype(BF16)


def _attn_sample(q, k_new, v_new, cache_k, cache_v, layer, page_table, bias_rows, t_new):
    n_seq, n_pages = page_table.shape
    page = cache_k.shape[3]
    page_spec = lambda j: pl.BlockSpec((None, None, ATT_WIDTH, page),
                                       lambda b, pt, j=j: (layer, pt[b, j], 0, 0))
    tok = pl.BlockSpec((t_new, ATT_WIDTH), lambda b, pt: (b, 0))
    seq_blk = pl.BlockSpec((None, t_new, ATT_WIDTH), lambda b, pt: (b, 0, 0))
    rows = N_HEADS * t_new
    whole = lambda w: pl.BlockSpec((rows, w), lambda b, pt: (0, 0))
    row_head = jnp.arange(rows, dtype=jnp.int32)[:, None] // t_new
    row_query = jnp.arange(rows, dtype=jnp.int32)[:, None] % t_new
    head_mask = (row_head == jnp.arange(ATT_WIDTH, dtype=jnp.int32)[None, :] // HEAD_DIM).astype(F32)
    new_mask = (jnp.arange(page, dtype=jnp.int32)[None, :] < row_query).astype(F32)
    kern = functools.partial(_attn_sample_kernel, n_pages=n_pages, t_new=t_new, page=page)
    grid_spec = pltpu.PrefetchScalarGridSpec(
        num_scalar_prefetch=1,
        grid=(n_seq,),
        in_specs=[whole(1), whole(ATT_WIDTH), whole(page), seq_blk, tok, tok]
                 + [page_spec(j) for j in range(n_pages)] * 2,
        out_specs=seq_blk,
    )
    o = pl.pallas_call(
        kern,
        grid_spec=grid_spec,
        out_shape=jax.ShapeDtypeStruct((n_seq, t_new, ATT_WIDTH), BF16),
        compiler_params=_params(("arbitrary",)),
        name="attn_sample",
    )(page_table, bias_rows, head_mask, new_mask, q.reshape(n_seq, t_new, ATT_WIDTH), k_new, v_new,
      *([cache_k] * n_pages), *([cache_v] * n_pages))
    return o.reshape(q.shape)


def _ssm_kernel(u_ref, d_ref, m_ref, wre_ref, wim_ref, vre_ref, vim_ref, a_ref, h0re_ref, h0im_ref,
                y_ref, hre_ref, him_ref, ug_s, yg_s, zre_s, zim_s, inre_s, inim_s, *, L, n_steps, bp):
    n_rows = n_steps * bp
    p = SSM_STATE
    gpb = GROUPS_PER_BLOCK
    slot = lax.shift_right_logical(lax.broadcasted_iota(jnp.int32, (1, LANES), 1), GROUP_CH.bit_length() - 1)

    def transpose_slots(a):
        a = list(a)
        k = gpb // 2
        while k:
            upper = jnp.bitwise_and(slot, k) != 0
            for i in range(gpb):
                if not i & k:
                    lo, hi = a[i], a[i + k]
                    a[i] = jnp.where(upper, pltpu.roll(hi, k * GROUP_CH, 1), lo)
                    a[i + k] = jnp.where(upper, hi, pltpu.roll(lo, LANES - k * GROUP_CH, 1))
            k //= 2
        return a

    rc = min(n_rows, SSM_ROW_CHUNK)
    n_half = L // gpb
    for r0 in range(0, n_rows, rc):
        for h in range(n_half):
            steps = [u_ref[pl.ds(r0 * L + h * gpb + j, rc, stride=L), :] for j in range(gpb)]
            for g, u_gh in enumerate(transpose_slots(steps)):
                ug_s[g, r0:r0 + rc, h * LANES:(h + 1) * LANES] = u_gh
    for g in range(gpb):
        u_g = ug_s[g]
        zre_s[:, g * p:(g + 1) * p] = _dot3(u_g, wre_ref[g])
        zim_s[:, g * p:(g + 1) * p] = _dot3(u_g, wim_ref[g])

    ar = a_ref[0:1, :]
    ai = a_ref[1:2, :]

    def step(k, carry):
        hr, hi = carry
        rows = pl.ds(k * bp, bp)
        inre_s[rows, :] = hr
        inim_s[rows, :] = hi
        zr = zre_s[rows, :]
        zi = zim_s[rows, :]
        return ar * hr - ai * hi + zr, ar * hi + ai * hr + zi

    hr, hi = lax.fori_loop(0, n_steps, step, (h0re_ref[...], h0im_ref[...]))
    hre_ref[...] = hr
    him_ref[...] = hi

    for g in range(gpb):
        yg_s[g] = (_dot3(ug_s[g], m_ref[g]) + _dot3(inre_s[:, g * p:(g + 1) * p], vre_ref[g])
                   + _dot3(inim_s[:, g * p:(g + 1) * p], vim_ref[g]))
    for r0 in range(0, n_rows, rc):
        for h in range(n_half):
            groups = [yg_s[g, r0:r0 + rc, h * LANES:(h + 1) * LANES] for g in range(gpb)]
            for j, y_step in enumerate(transpose_slots(groups)):
                rows = pl.ds(r0 * L + h * gpb + j, rc, stride=L)
                y_ref[rows, :] = y_step + d_ref[...] * u_ref[rows, :]


def _ssm_mats(a_re, a_im, log_dt, b_re, b_im, c_re, c_im, lengths):
    hp = lax.Precision.HIGHEST
    L = max(lengths)
    dt = jnp.exp(log_dt)[:, None]
    mag = jnp.exp(a_re * dt)
    ab_re, ab_im = mag * jnp.cos(a_im * dt), mag * jnp.sin(a_im * dt)
    den = a_re * a_re + a_im * a_im
    nr, ni = ab_re - 1.0, ab_im
    f_re = (nr * a_re + ni * a_im) / den
    f_im = (ni * a_re - nr * a_im) / den
    bb_re = f_re[..., None] * b_re - f_im[..., None] * b_im
    bb_im = f_re[..., None] * b_im + f_im[..., None] * b_re
    pr, pi = [jnp.ones_like(ab_re)], [jnp.zeros_like(ab_re)]
    for _ in range(L):
        pr, pi = pr + [ab_re * pr[-1] - ab_im * pi[-1]], pi + [ab_re * pi[-1] + ab_im * pr[-1]]
    p_re, p_im = jnp.stack(pr), jnp.stack(pi)
    abr = p_re[:L, :, :, None] * bb_re - p_im[:L, :, :, None] * bb_im
    abi = p_re[:L, :, :, None] * bb_im + p_im[:L, :, :, None] * bb_re
    kern = (jnp.einsum('gcp,tgpd->tgcd', c_re, abr, precision=hp)
            - jnp.einsum('gcp,tgpd->tgcd', c_im, abi, precision=hp))
    idx = jnp.arange(L)
    delay = (idx[None, None, :] - idx[None, :, None] == idx[:, None, None]).astype(F32)
    flip = (idx[:, None] + idx[None, :] == L - 1).astype(F32)
    lc = L * GROUP_CH
    m = jnp.einsum('zst,zgcd->gsdtc', delay, kern, precision=hp).reshape(N_GROUPS, lc, lc)
    w_re = jnp.einsum('sz,zgpd->gsdp', flip, abr, precision=hp).reshape(N_GROUPS, lc, SSM_STATE)
    w_im = jnp.einsum('sz,zgpd->gsdp', flip, abi, precision=hp).reshape(N_GROUPS, lc, SSM_STATE)
    cr = jnp.transpose(c_re, (0, 2, 1))[:, :, None, :]
    ci = jnp.transpose(c_im, (0, 2, 1))[:, :, None, :]
    qr = jnp.transpose(p_re[1:], (1, 2, 0))[..., None]
    qi = jnp.transpose(p_im[1:], (1, 2, 0))[..., None]
    v_re = (cr * qr - ci * qi).reshape(N_GROUPS, SSM_STATE, lc)
    v_im = (-cr * qi - ci * qr).reshape(N_GROUPS, SSM_STATE, lc)

    def for_length(n):
        nc, skip = n * GROUP_CH, (L - n) * GROUP_CH
        a_n = jnp.stack([p_re[n], p_im[n]], axis=1)
        return (m[:, :nc, :nc], w_re[:, skip:], w_im[:, skip:], v_re[:, :, :nc], v_im[:, :, :nc], a_n)

    return {n: for_length(n) for n in lengths}


def _ssm(u, h0, mats, d_skip, batch, seq, L):
    m, w_re, w_im, v_re, v_im, a_l = mats
    gpb = GROUPS_PER_BLOCK
    n_cb = N_GROUPS // gpb
    lc = L * GROUP_CH
    state_w = gpb * SSM_STATE
    if seq == L:
        n_bblk, bp, n_steps = 1, batch, 1
    else:
        n_bblk, bp, n_steps = batch, 1, seq // L
    tok = bp * n_steps * L
    n_rows = bp * n_steps

    def to_blocks(x):
        x = x.reshape(n_bblk, bp, n_cb, state_w)
        return jnp.transpose(x, (0, 2, 1, 3))

    if h0 is None:
        h0re = h0im = jnp.zeros((n_bblk, n_cb, bp, state_w), F32)
    else:
        h0re, h0im = to_blocks(h0[..., 0]), to_blocks(h0[..., 1])
    a_blk = jnp.transpose(a_l.reshape(n_cb, gpb, 2, SSM_STATE), (0, 2, 1, 3)).reshape(n_cb, 2, state_w)

    grp = lambda a, b: pl.BlockSpec((gpb, a, b), lambda i, cb: (cb, 0, 0))
    act = pl.BlockSpec((tok, LANES), lambda i, cb: (i, cb))
    state = pl.BlockSpec((None, None, bp, state_w), lambda i, cb: (i, cb, 0, 0))
    state_shape = jax.ShapeDtypeStruct((n_bblk, n_cb, bp, state_w), F32)
    kern = functools.partial(_ssm_kernel, L=L, n_steps=n_steps, bp=bp)
    y, hre, him = pl.pallas_call(
        kern,
        grid=(n_bblk, n_cb),
        in_specs=[act, pl.BlockSpec((1, LANES), lambda i, cb: (0, cb)),
                  grp(lc, lc), grp(lc, SSM_STATE), grp(lc, SSM_STATE),
                  grp(SSM_STATE, lc), grp(SSM_STATE, lc),
                  pl.BlockSpec((None, 2, state_w), lambda i, cb: (cb, 0, 0)), state, state],
        out_specs=[act, state, state],
        out_shape=[jax.ShapeDtypeStruct(u.shape, F32), state_shape, state_shape],
        scratch_shapes=[pltpu.VMEM((gpb, n_rows, lc), F32)] * 2 + [pltpu.VMEM((n_rows, state_w), F32)] * 4,
        compiler_params=_params(("parallel", "parallel")),
        name="ssm",
    )(u, d_skip.reshape(1, D_SSM), m, w_re, w_im, v_re, v_im, a_blk, h0re, h0im)

    def from_blocks(x):
        return jnp.transpose(x, (0, 2, 1, 3)).reshape(batch, N_GROUPS, SSM_STATE)

    return y, jnp.stack([from_blocks(hre), from_blocks(him)], axis=-1)


def _post_kernel(x_ref, o_ref, y_ref, ga_ref, gs_ref, p_ref,
                 wao_ref, wgv_ref, wgg_ref, wout_ref, gmlp_ref, wup_ref, wdown_ref,
                 gple_ref, wple_ref, wpg_ref, out_ref, *, ff_chunk):
    def rms(x, g_ref):
        ms = jnp.mean(x * x, axis=-1, keepdims=True)
        return (x * lax.rsqrt(ms + RMS_EPS) * g_ref[...]).astype(BF16)

    y_act = jax.nn.gelu(y_ref[...], approximate=True).astype(BF16)
    attn_out = _dot(o_ref[...], wao_ref[...])
    ssm_out = _dot(y_act, wgv_ref[...]) * jax.nn.sigmoid(_dot(y_act, wgg_ref[...]))
    merged = ga_ref[...].astype(F32) * attn_out + gs_ref[...].astype(F32) * ssm_out
    x = x_ref[...] + _dot(merged.astype(BF16), wout_ref[...])
    h2 = rms(x, gmlp_ref)
    d_ff = wup_ref.shape[1]
    mlp = None
    for c0 in range(0, d_ff, ff_chunk):
        hid = jnp.square(jnp.maximum(_dot(h2, wup_ref[:, c0:c0 + ff_chunk]), 0.0)).astype(BF16)
        part = _dot(hid, wdown_ref[c0:c0 + ff_chunk, :])
        mlp = part if mlp is None else mlp + part
    x = x + mlp
    h3 = rms(x, gple_ref)
    out_ref[...] = x + _dot(p_ref[...].astype(BF16), wple_ref[...]) * jax.nn.sigmoid(_dot(h3, wpg_ref[...]))


def _post(x, o, y, ga, gs, p, w):
    m, d = x.shape
    tm = min(ROW_TILE, m)
    row = lambda a: pl.BlockSpec((tm, a.shape[1]), lambda i: (i, 0))
    vec = lambda a: a.reshape(1, -1)
    acts = (x, o, y, ga, gs, p)
    consts = (w["w_attn_out"], w["w_glu_val"], w["w_glu_gate"], w["w_out"], vec(w["g_mlp"]),
              w["w_up"], w["w_down"], vec(w["g_ple"]), w["w_ple"], w["w_ple_gate"])
    return pl.pallas_call(
        functools.partial(_post_kernel, ff_chunk=1024),
        grid=(m // tm,),
        in_specs=[row(a) for a in acts] + [_resident(c.shape) for c in consts],
        out_specs=pl.BlockSpec((tm, d), lambda i: (i, 0)),
        out_shape=jax.ShapeDtypeStruct((m, d), F32),
        compiler_params=_params(("parallel",)),
        name="post",
    )(*acts, *consts)


def _layer(x, p_l, attend, h0, batch, seq, chunk, w):
    m = batch * seq
    q, k, v, u, ga, gs = _in_proj(x.reshape(m, -1), w["g_mix"], w["w_in"], w["bd"], w["g_q"], w["g_k"])
    o = attend(q, k, v)
    y, h_last = _ssm(u, h0, w["ssm_mats"][chunk], w["ssm_d"], batch, seq, chunk)
    x_out = _post(x.reshape(m, -1), o, y, ga, gs, p_l.reshape(m, -1), w)
    shape4 = (batch, seq, N_HEADS, HEAD_DIM)
    return x_out.reshape(x.shape), k.reshape(shape4), v.reshape(shape4), h_last


def kernel(x_prompt, x_sample, p_prompt, p_sample, cache_k, cache_v, state_ssm, page_table, g_mix, w_in, g_q, g_k, sb_bias, w_attn_out, ssm_a_re, ssm_a_im, ssm_log_dt, ssm_b_re, ssm_b_im, ssm_c_re, ssm_c_im, ssm_d, w_glu_val, w_glu_gate, w_out, g_mlp, w_up, w_down, g_ple, w_ple, w_ple_gate):
    depth = w_in.shape[0]
    batch, seq, _ = x_prompt.shape
    dec_batch, dec_seq, _ = x_sample.shape
    to_pages = lambda c: jnp.transpose(c, (0, 1, 3, 4, 2)).reshape(c.shape[:2] + (ATT_WIDTH, c.shape[2]))
    ck, cv = to_pages(cache_k), to_pages(cache_v)
    hd = jnp.arange(ATT_WIDTH, dtype=jnp.int32) // HEAD_DIM
    bd = jnp.where(hd[:, None] == hd[None, :], 1.0 / HEAD_DIM, 0.0).astype(BF16)

    xp, xs = x_prompt, x_sample
    outs = [[] for _ in range(6)]
    for i in range(depth):
        ssm_p = (ssm_a_re[i], ssm_a_im[i], ssm_log_dt[i], ssm_b_re[i], ssm_b_im[i], ssm_c_re[i], ssm_c_im[i])
        w = {
            "g_mix": g_mix[i], "w_in": w_in[i].astype(BF16), "bd": bd,
            "g_q": jnp.tile(g_q[i], N_HEADS).reshape(1, ATT_WIDTH),
            "g_k": jnp.tile(g_k[i], N_HEADS).reshape(1, ATT_WIDTH),
            "w_attn_out": w_attn_out[i].astype(BF16), "w_glu_val": w_glu_val[i].astype(BF16),
            "w_glu_gate": w_glu_gate[i].astype(BF16), "w_out": w_out[i].astype(BF16),
            "g_mlp": g_mlp[i], "w_up": w_up[i].astype(BF16), "w_down": w_down[i].astype(BF16),
            "g_ple": g_ple[i], "w_ple": w_ple[i].astype(BF16), "w_ple_gate": w_ple_gate[i].astype(BF16),
            "ssm_d": ssm_d[i],
            "ssm_mats": _ssm_mats(*ssm_p, sorted({PROMPT_CHUNK, dec_seq})),
        }
        bias = sb_bias[i].astype(F32)
        bias_rows = jnp.repeat(bias, dec_seq).reshape(N_HEADS * dec_seq, 1)

        attend_p = lambda q, k, v: _attn_prompt(q, k, v, bias, batch, seq)
        xp, kp, vp, hp = _layer(xp, p_prompt[i], attend_p, None, batch, seq, PROMPT_CHUNK, w)
        attend_s = lambda q, k, v: _attn_sample(q, k, v, ck, cv, i, page_table, bias_rows, dec_seq)
        xs, ks, vs, hs = _layer(xs, p_sample[i], attend_s, state_ssm[i], dec_batch, dec_seq, dec_seq, w)
        for lst, val in zip(outs, (kp, vp, hp, ks, vs, hs)):
            lst.append(val)
    k_p, v_p, s_p, k_s, v_s, s_s = (jnp.stack(o) for o in outs)
    return xp, xs, k_p, v_p, s_p, k_s, v_s, s_s
```

```python
import functools

import jax
import jax.numpy as jnp
from jax import lax
from jax.experimental import pallas as pl
from jax.experimental.pallas import tpu as pltpu

F32 = jnp.float32
BF16 = jnp.bfloat16

N_HEADS = 8
HEAD_DIM = 64
ATT_WIDTH = N_HEADS * HEAD_DIM
GROUP_CH = 16
N_GROUPS = 32
SSM_STATE = 64
D_SSM = N_GROUPS * GROUP_CH
RMS_EPS = 1e-6
LOG2E = 1.4426950408889634
Q_SCALE = -(HEAD_DIM ** -0.5) * LOG2E
LANES = 128
MXU_TILE = 256
GROUPS_PER_BLOCK = LANES // GROUP_CH
VMEM_LIMIT = 56 * 1024 * 1024

PROMPT_CHUNK = 16
SSM_ROW_CHUNK = 64
ATT_TK = 256
ATT_TQ = 2 * ATT_TK
ROW_TILE = 256


def _dot(a, b):
    return jnp.dot(a, b, preferred_element_type=F32)


def _dot_nt(a, b):
    return lax.dot_general(a, b, (((1,), (1,)), ((), ())), preferred_element_type=F32)


def _split(x):
    hi = x.astype(BF16)
    lo = (x - hi.astype(F32)).astype(BF16)
    return hi, lo


def _dot3(a, b):
    ah, al = _split(a)
    bh, bl = _split(b)
    return _dot(ah, bh) + _dot(al, bh) + _dot(ah, bl)


def _params(sem):
    return pltpu.CompilerParams(dimension_semantics=sem, vmem_limit_bytes=VMEM_LIMIT)


def _resident(shape):
    nd = len(shape)
    return pl.BlockSpec(shape, lambda *_: (0,) * nd, pipeline_mode=pl.Buffered(1))


def _in_proj_kernel(x_ref, gmix_ref, w_ref, bd_ref, gq_ref, gk_ref, *rest, kv_transposed, n_prev):
    x = x_ref[...]
    ms = jnp.mean(x * x, axis=-1, keepdims=True)
    h = (x * lax.rsqrt(ms + RMS_EPS) * gmix_ref[...]).astype(BF16)
    a = ATT_WIDTH

    def head_norm(p, g_ref):
        msh = _dot((p * p).astype(BF16), bd_ref[...])
        return p * lax.rsqrt(msh + RMS_EPS) * g_ref[...]

    if kv_transposed:
        wkvt_ref = rest[0]
        q_ref, k_ref, v_ref, u_ref, ga_ref, gs_ref = rest[1 + n_prev:]
        kvt = jnp.concatenate([_dot_nt(wkvt_ref[r:r + MXU_TILE, :], h) for r in range(0, 2 * a, MXU_TILE)], axis=0)
        kt = kvt[:a]
        mst = _dot(bd_ref[...], (kt * kt).astype(BF16))
        k_ref[...] = kt * lax.rsqrt(mst + RMS_EPS) * gk_ref[...]
        v_ref[...] = kvt[a:]
    else:
        q_ref, k_ref, v_ref, u_ref, ga_ref, gs_ref = rest
        k_ref[...] = head_norm(_dot(h, w_ref[:, a:2 * a]), gk_ref)
        v_ref[...] = _dot(h, w_ref[:, 2 * a:3 * a])
    q_ref[...] = (head_norm(_dot(h, w_ref[:, 0:a]), gq_ref) * Q_SCALE).astype(BF16)
    u_ref[...] = _dot(h, w_ref[:, 3 * a:3 * a + D_SSM])
    o = 3 * a + D_SSM
    d = ga_ref.shape[-1]
    ga_ref[...] = jax.nn.sigmoid(_dot(h, w_ref[:, o:o + d])).astype(BF16)
    gs_ref[...] = jax.nn.sigmoid(_dot(h, w_ref[:, o + d:o + 2 * d])).astype(BF16)


def _in_proj(x, w, kv_slot=None):
    m, d = x.shape
    tm = min(ROW_TILE, m)
    row = lambda width: pl.BlockSpec((tm, width), lambda i: (i, 0))
    tok = lambda width, t: (row(width), jax.ShapeDtypeStruct((m, width), t))
    ins = [x, w["g_mix"].reshape(1, d), w["w_in"], w["bd"], w["g_q"]]
    in_specs = [row(d), _resident((1, d)), _resident(w["w_in"].shape), _resident(w["bd"].shape),
                _resident((1, ATT_WIDTH))]
    aliases = {}
    n_prev = 0
    if kv_slot is None:
        ins.append(w["g_k"])
        in_specs.append(_resident((1, ATT_WIDTH)))
        kv = [tok(ATT_WIDTH, F32)] * 2
    else:
        layer, depth, batch, seq, prev_kt, prev_vt = kv_slot
        per_seq = seq // tm
        ins += [w["g_k"].reshape(ATT_WIDTH, 1), w["w_kvt"]]
        in_specs += [_resident((ATT_WIDTH, 1)), _resident(w["w_kvt"].shape)]
        if prev_kt is not None:
            n_prev = 2
            aliases = {len(ins): 1, len(ins) + 1: 2}
            ins += [prev_kt, prev_vt]
            in_specs += [pl.BlockSpec(memory_space=pl.ANY)] * 2
        slot = pl.BlockSpec((None, None, ATT_WIDTH, tm), lambda i: (layer, i // per_seq, 0, i % per_seq))
        kv = [(slot, jax.ShapeDtypeStruct((depth, batch, ATT_WIDTH, seq), F32))] * 2
    outs = [tok(ATT_WIDTH, BF16)] + kv + [tok(D_SSM, F32), tok(d, BF16), tok(d, BF16)]
    return pl.pallas_call(
        functools.partial(_in_proj_kernel, kv_transposed=kv_slot is not None, n_prev=n_prev),
        grid=(m // tm,),
        in_specs=in_specs,
        out_specs=[o[0] for o in outs],
        out_shape=[o[1] for o in outs],
        input_output_aliases=aliases,
        compiler_params=_params(("parallel",)),
        name="in_proj",
    )(*ins)


def _sb_block(nz, mask, tri, carry):
    m = jnp.minimum(nz, 0.0)
    e = jnp.exp2(m + (m - nz))
    ls = m - jnp.log(1.0 + e) * LOG2E
    if mask is not None:
        ls = jnp.where(mask, ls, 0.0)
    if tri.shape[0] == tri.shape[1]:
        parts = ls.astype(BF16)
    else:
        parts = jnp.concatenate(_split(ls), axis=1)
    later = _dot(parts, tri)
    w = jnp.exp2((ls - nz) + (later + carry))
    if mask is not None:
        w = jnp.where(mask, w, 0.0)
    return w, carry + (later[:, 0:1] + ls[:, 0:1])


def _later_keys(n, copies):
    assert copies in (1, 2)
    r = lax.broadcasted_iota(jnp.int32, (copies * n, n), 0)
    c = lax.broadcasted_iota(jnp.int32, (copies * n, n), 1)
    return jnp.where(jnp.where(r >= n, r - n, r) > c, 1.0, 0.0).astype(BF16)


def _attn_prompt_kernel(bias_ref, q_ref, kt_ref, vt_ref, o_ref, kb_s, vb_s, acc_ref, carry_ref,
                        *, tq, tk, seq):
    hp = pl.program_id(1)
    qi = pl.program_id(2)
    lane = lax.broadcasted_iota(jnp.int32, (1, LANES), 1)
    first_head = lane < HEAD_DIM

    @pl.when(qi == 0)
    def _stage_keys_values():
        for i in range(seq // tk):
            kb_s[i] = kt_ref[:, i * tk:(i + 1) * tk].astype(BF16)
            vb_s[i] = vt_ref[:, i * tk:(i + 1) * tk].astype(BF16)

    qs = q_ref[...]
    q2 = jnp.concatenate([jnp.where(first_head, qs, 0), jnp.where(first_head, 0, qs)], axis=0)
    nbias = jnp.concatenate([jnp.full((tq, 1), bias_ref[2 * hp + h] * (-LOG2E), F32) for h in range(2)], axis=0)
    tri = _later_keys(tk, 1)
    r = lax.broadcasted_iota(jnp.int32, (2 * tq, tk), 0)
    c = lax.broadcasted_iota(jnp.int32, (2 * tq, tk), 1)
    q_pos = jnp.where(r < tq, r, r - tq) + qi * tq

    def block(kb, mask, carry):
        nz = _dot(q2, kb_s[kb]) + nbias
        w, carry = _sb_block(nz, mask, tri, carry)
        pv = _dot_nt(w.astype(BF16), vb_s[kb])
        return jnp.where(first_head, pv[:tq], pv[tq:]), carry

    def run(blocks, masked, carry):
        total = None
        for kb in blocks:
            contrib, carry = block(kb, (c + kb * tk < q_pos) if masked else None, carry)
            total = contrib if total is None else total + contrib
        return total, carry

    per_q = tq // tk
    top = per_q * (qi + 1) - 1
    total, carry = run([top - d for d in range(per_q)], True, jnp.zeros((2 * tq, 1), F32))
    acc_ref[...] = total
    carry_ref[...] = carry

    def below_diagonal(i, _):
        first = top - per_q * (i + 1)
        total, carry = run([first - d for d in range(per_q)], False, carry_ref[...])
        acc_ref[...] += total
        carry_ref[...] = carry
        return 0
    lax.fori_loop(0, qi, below_diagonal, 0)
    o_ref[...] = acc_ref[...].astype(BF16)


def _attn_prompt(q, kt_all, vt_all, layer, bias, batch, seq):
    tq, tk = ATT_TQ, ATT_TK
    nq = seq // tq
    kern = functools.partial(_attn_prompt_kernel, tq=tq, tk=tk, seq=seq)
    head_pair = pl.BlockSpec((None, None, LANES, seq), lambda b, hp, i: (layer, b, hp, 0))
    return pl.pallas_call(
        kern,
        grid=(batch, N_HEADS // 2, nq),
        in_specs=[pl.BlockSpec(memory_space=pltpu.SMEM),
                  pl.BlockSpec((tq, LANES), lambda b, hp, i: (b * nq + i, hp)),
                  head_pair, head_pair],
        out_specs=pl.BlockSpec((tq, LANES), lambda b, hp, i: (b * nq + i, hp)),
        out_shape=jax.ShapeDtypeStruct(q.shape, BF16),
        scratch_shapes=[pltpu.VMEM((seq // tk, LANES, tk), BF16)] * 2
                       + [pltpu.VMEM((tq, LANES), F32), pltpu.VMEM((2 * tq, 1), F32)],
        compiler_params=_params(("parallel", "parallel", "arbitrary")),
        name="attn_prompt",
    )(bias, q, kt_all, vt_all)


def _attn_sample_kernel(pt_ref, bias_ref, hmask_ref, nmask_ref, q_ref, kn_ref, vn_ref, *rest,
                        n_pages, t_new, page):
    del pt_ref
    k_pages = rest[:n_pages]
    v_pages = rest[n_pages:2 * n_pages]
    o_ref = rest[2 * n_pages]
    rows = N_HEADS * t_new
    same_head = hmask_ref[...] > 0.5
    q = q_ref[...].astype(F32)
    q_bd = jnp.where(same_head, jnp.concatenate([q] * N_HEADS, axis=0), 0.0).astype(BF16)
    nbias = bias_ref[...] * (-LOG2E)

    pad = jnp.zeros((page - t_new, ATT_WIDTH), F32)
    k_new = jnp.concatenate([kn_ref[...], pad], axis=0).astype(BF16)
    v_new = jnp.concatenate([vn_ref[...], pad], axis=0).astype(BF16)
    new_mask = nmask_ref[...] > 0.5
    nz = _dot_nt(q_bd, k_new) + nbias
    w, carry = _sb_block(nz, new_mask, _later_keys(page, 2), jnp.zeros((rows, 1), F32))
    acc = _dot(w.astype(BF16), v_new)

    per_block = 2 if n_pages % 2 == 0 else 1
    tri = _later_keys(per_block * page, 2)

    def key_block(page_refs, jb):
        pages = [page_refs[jb * per_block + i][...] for i in range(per_block)]
        return (pages[0] if per_block == 1 else jnp.concatenate(pages, axis=1)).astype(BF16)

    for jb in reversed(range(n_pages // per_block)):
        nz = _dot(q_bd, key_block(k_pages, jb)) + nbias
        w, carry = _sb_block(nz, None, tri, carry)
        acc = acc + _dot_nt(w.astype(BF16), key_block(v_pages, jb))
    acc = jnp.where(same_head, acc, 0.0)
    out = acc[0:t_new, :]
    for h in range(1, N_HEADS):
        out = out + acc[h * t_new:(h + 1) * t_new, :]
    o_ref[...] = out.astype(BF16)


def _attn_sample(q, k_new, v_new, cache_k, cache_v, layer, page_table, bias_rows, t_new):
    n_seq, n_pages = page_table.shape
    page = cache_k.shape[3]
    page_spec = lambda j: pl.BlockSpec((None, None, ATT_WIDTH, page),
                                       lambda b, pt, j=j: (layer, pt[b, j], 0, 0))
    tok = pl.BlockSpec((t_new, ATT_WIDTH), lambda b, pt: (b, 0))
    seq_blk = pl.BlockSpec((None, t_new, ATT_WIDTH), lambda b, pt: (b, 0, 0))
    rows = N_HEADS * t_new
    whole = lambda w: pl.BlockSpec((rows, w), lambda b, pt: (0, 0))
    row_head = jnp.arange(rows, dtype=jnp.int32)[:, None] // t_new
    row_query = jnp.arange(rows, dtype=jnp.int32)[:, None] % t_new
    head_mask = (row_head == jnp.arange(ATT_WIDTH, dtype=jnp.int32)[None, :] // HEAD_DIM).astype(F32)
    new_mask = (jnp.arange(page, dtype=jnp.int32)[None, :] < row_query).astype(F32)
    kern = functools.partial(_attn_sample_kernel, n_pages=n_pages, t_new=t_new, page=page)
    grid_spec = pltpu.PrefetchScalarGridSpec(
        num_scalar_prefetch=1,
        grid=(n_seq,),
        in_specs=[whole(1), whole(ATT_WIDTH), whole(page), seq_blk, tok, tok]
                 + [page_spec(j) for j in range(n_pages)] * 2,
        out_specs=seq_blk,
    )
    o = pl.pallas_call(
        kern,
        grid_spec=grid_spec,
        out_shape=jax.ShapeDtypeStruct((n_seq, t_new, ATT_WIDTH), BF16),
        compiler_params=_params(("arbitrary",)),
        name="attn_sample",
    )(page_table, bias_rows, head_mask, new_mask, q.reshape(n_seq, t_new, ATT_WIDTH), k_new, v_new,
      *([cache_k] * n_pages), *([cache_v] * n_pages))
    return o.reshape(q.shape)


def _ssm_kernel(u_ref, d_ref, m_ref, wre_ref, wim_ref, vre_ref, vim_ref, a_ref, h0re_ref, h0im_ref,
                y_ref, hre_ref, him_ref, ug_s, yg_s, zre_s, zim_s, inre_s, inim_s, *, L, n_steps, bp):
    n_rows = n_steps * bp
    p = SSM_STATE
    gpb = GROUPS_PER_BLOCK
    slot = lax.shift_right_logical(lax.broadcasted_iota(jnp.int32, (1, LANES), 1), GROUP_CH.bit_length() - 1)

    def transpose_slots(a):
        a = list(a)
        k = gpb // 2
        while k:
            upper = jnp.bitwise_and(slot, k) != 0
            for i in range(gpb):
                if not i & k:
                    lo, hi = a[i], a[i + k]
                    a[i] = jnp.where(upper, pltpu.roll(hi, k * GROUP_CH, 1), lo)
                    a[i + k] = jnp.where(upper, hi, pltpu.roll(lo, LANES - k * GROUP_CH, 1))
            k //= 2
        return a

    rc = min(n_rows, SSM_ROW_CHUNK)
    n_half = L // gpb
    for r0 in range(0, n_rows, rc):
        for h in range(n_half):
            steps = [u_ref[pl.ds(r0 * L + h * gpb + j, rc, stride=L), :] for j in range(gpb)]
            for g, u_gh in enumerate(transpose_slots(steps)):
                ug_s[g, r0:r0 + rc, h * LANES:(h + 1) * LANES] = u_gh
    for g in range(gpb):
        u_g = ug_s[g]
        zre_s[:, g * p:(g + 1) * p] = _dot3(u_g, wre_ref[g])
        zim_s[:, g * p:(g + 1) * p] = _dot3(u_g, wim_ref[g])

    ar = a_ref[0:1, :]
    ai = a_ref[1:2, :]

    def step(k, carry):
        hr, hi = carry
        rows = pl.ds(k * bp, bp)
        inre_s[rows, :] = hr
        inim_s[rows, :] = hi
        zr = zre_s[rows, :]
        zi = zim_s[rows, :]
        return ar * hr - ai * hi + zr, ar * hi + ai * hr + zi

    hr, hi = lax.fori_loop(0, n_steps, step, (h0re_ref[...], h0im_ref[...]))
    hre_ref[...] = hr
    him_ref[...] = hi

    for g in range(gpb):
        yg_s[g] = (_dot3(ug_s[g], m_ref[g]) + _dot3(inre_s[:, g * p:(g + 1) * p], vre_ref[g])
                   + _dot3(inim_s[:, g * p:(g + 1) * p], vim_ref[g]))
    for r0 in range(0, n_rows, rc):
        for h in range(n_half):
            groups = [yg_s[g, r0:r0 + rc, h * LANES:(h + 1) * LANES] for g in range(gpb)]
            for j, y_step in enumerate(transpose_slots(groups)):
                rows = pl.ds(r0 * L + h * gpb + j, rc, stride=L)
                y_ref[rows, :] = y_step + d_ref[...] * u_ref[rows, :]


def _ssm_mats(a_re, a_im, log_dt, b_re, b_im, c_re, c_im, lengths):
    hp = lax.Precision.HIGHEST
    L = max(lengths)
    dt = jnp.exp(log_dt)[:, None]
    mag = jnp.exp(a_re * dt)
    ab_re, ab_im = mag * jnp.cos(a_im * dt), mag * jnp.sin(a_im * dt)
    den = a_re * a_re + a_im * a_im
    nr, ni = ab_re - 1.0, ab_im
    f_re = (nr * a_re + ni * a_im) / den
    f_im = (ni * a_re - nr * a_im) / den
    bb_re = f_re[..., None] * b_re - f_im[..., None] * b_im
    bb_im = f_re[..., None] * b_im + f_im[..., None] * b_re
    pr, pi = [jnp.ones_like(ab_re)], [jnp.zeros_like(ab_re)]
    for _ in range(L):
        pr, pi = pr + [ab_re * pr[-1] - ab_im * pi[-1]], pi + [ab_re * pi[-1] + ab_im * pr[-1]]
    p_re, p_im = jnp.stack(pr), jnp.stack(pi)
    abr = p_re[:L, :, :, None] * bb_re - p_im[:L, :, :, None] * bb_im
    abi = p_re[:L, :, :, None] * bb_im + p_im[:L, :, :, None] * bb_re
    kern = (jnp.einsum('gcp,tgpd->tgcd', c_re, abr, precision=hp)
            - jnp.einsum('gcp,tgpd->tgcd', c_im, abi, precision=hp))
    idx = jnp.arange(L)
    delay = (idx[None, None, :] - idx[None, :, None] == idx[:, None, None]).astype(F32)
    flip = (idx[:, None] + idx[None, :] == L - 1).astype(F32)
    lc = L * GROUP_CH
    m = jnp.einsum('zst,zgcd->gsdtc', delay, kern, precision=hp).reshape(N_GROUPS, lc, lc)
    w_re = jnp.einsum('sz,zgpd->gsdp', flip, abr, precision=hp).reshape(N_GROUPS, lc, SSM_STATE)
    w_im = jnp.einsum('sz,zgpd->gsdp', flip, abi, precision=hp).reshape(N_GROUPS, lc, SSM_STATE)
    cr = jnp.transpose(c_re, (0, 2, 1))[:, :, None, :]
    ci = jnp.transpose(c_im, (0, 2, 1))[:, :, None, :]
    qr = jnp.transpose(p_re[1:], (1, 2, 0))[..., None]
    qi = jnp.transpose(p_im[1:], (1, 2, 0))[..., None]
    v_re = (cr * qr - ci * qi).reshape(N_GROUPS, SSM_STATE, lc)
    v_im = (-cr * qi - ci * qr).reshape(N_GROUPS, SSM_STATE, lc)

    def for_length(n):
        nc, skip = n * GROUP_CH, (L - n) * GROUP_CH
        a_n = jnp.stack([p_re[n], p_im[n]], axis=1)
        return (m[:, :nc, :nc], w_re[:, skip:], w_im[:, skip:], v_re[:, :, :nc], v_im[:, :, :nc], a_n)

    return {n: for_length(n) for n in lengths}


def _ssm(u, h0, mats, layer, d_skip, batch, seq, L):
    m, w_re, w_im, v_re, v_im, a_l = mats
    depth = m.shape[0]
    gpb = GROUPS_PER_BLOCK
    n_cb = N_GROUPS // gpb
    lc = L * GROUP_CH
    state_w = gpb * SSM_STATE
    if seq == L:
        n_bblk, bp, n_steps = 1, batch, 1
    else:
        n_bblk, bp, n_steps = batch, 1, seq // L
    tok = bp * n_steps * L
    n_rows = bp * n_steps

    def to_blocks(x):
        x = x.reshape(n_bblk, bp, n_cb, state_w)
        return jnp.transpose(x, (0, 2, 1, 3))

    if h0 is None:
        h0re = h0im = jnp.zeros((n_bblk, n_cb, bp, state_w), F32)
    else:
        h0re, h0im = to_blocks(h0[..., 0]), to_blocks(h0[..., 1])
    a_blk = jnp.transpose(a_l.reshape(depth, n_cb, gpb, 2, SSM_STATE), (0, 1, 3, 2, 4))
    a_blk = a_blk.reshape(depth, n_cb, 2, state_w)

    grp = lambda a, b: pl.BlockSpec((None, gpb, a, b), lambda i, cb: (layer, cb, 0, 0))
    act = pl.BlockSpec((tok, LANES), lambda i, cb: (i, cb))
    state = pl.BlockSpec((None, None, bp, state_w), lambda i, cb: (i, cb, 0, 0))
    state_shape = jax.ShapeDtypeStruct((n_bblk, n_cb, bp, state_w), F32)
    kern = functools.partial(_ssm_kernel, L=L, n_steps=n_steps, bp=bp)
    y, hre, him = pl.pallas_call(
        kern,
        grid=(n_bblk, n_cb),
        in_specs=[act, pl.BlockSpec((1, LANES), lambda i, cb: (0, cb)),
                  grp(lc, lc), grp(lc, SSM_STATE), grp(lc, SSM_STATE),
                  grp(SSM_STATE, lc), grp(SSM_STATE, lc),
                  pl.BlockSpec((None, None, 2, state_w), lambda i, cb: (layer, cb, 0, 0)), state, state],
        out_specs=[act, state, state],
        out_shape=[jax.ShapeDtypeStruct(u.shape, F32), state_shape, state_shape],
        scratch_shapes=[pltpu.VMEM((gpb, n_rows, lc), F32)] * 2 + [pltpu.VMEM((n_rows, state_w), F32)] * 4,
        compiler_params=_params(("parallel", "parallel")),
        name="ssm",
    )(u, d_skip.reshape(1, D_SSM), m, w_re, w_im, v_re, v_im, a_blk, h0re, h0im)

    def from_blocks(x):
        return jnp.transpose(x, (0, 2, 1, 3)).reshape(batch, N_GROUPS, SSM_STATE)

    return y, jnp.stack([from_blocks(hre), from_blocks(him)], axis=-1)


def _post_kernel(x_ref, o_ref, y_ref, ga_ref, gs_ref, p_ref,
                 wao_ref, wgv_ref, wgg_ref, wout_ref, gmlp_ref, wup_ref, wdown_ref,
                 gple_ref, wple_ref, wpg_ref, out_ref, *, ff_chunk):
    def rms(x, g_ref):
        ms = jnp.mean(x * x, axis=-1, keepdims=True)
        return (x * lax.rsqrt(ms + RMS_EPS) * g_ref[...]).astype(BF16)

    y_act = jax.nn.gelu(y_ref[...], approximate=True).astype(BF16)
    attn_out = _dot(o_ref[...], wao_ref[...])
    ssm_out = _dot(y_act, wgv_ref[...]) * jax.nn.sigmoid(_dot(y_act, wgg_ref[...]))
    merged = ga_ref[...].astype(F32) * attn_out + gs_ref[...].astype(F32) * ssm_out
    x = x_ref[...] + _dot(merged.astype(BF16), wout_ref[...])
    h2 = rms(x, gmlp_ref)
    d_ff = wup_ref.shape[1]
    mlp = None
    for c0 in range(0, d_ff, ff_chunk):
        hid = jnp.square(jnp.maximum(_dot(h2, wup_ref[:, c0:c0 + ff_chunk]), 0.0)).astype(BF16)
        part = _dot(hid, wdown_ref[c0:c0 + ff_chunk, :])
        mlp = part if mlp is None else mlp + part
    x = x + mlp
    h3 = rms(x, gple_ref)
    out_ref[...] = x + _dot(p_ref[...].astype(BF16), wple_ref[...]) * jax.nn.sigmoid(_dot(h3, wpg_ref[...]))


def _post(x, o, y, ga, gs, p, w):
    m, d = x.shape
    tm = min(ROW_TILE, m)
    row = lambda a: pl.BlockSpec((tm, a.shape[1]), lambda i: (i, 0))
    vec = lambda a: a.reshape(1, -1)
    acts = (x, o, y, ga, gs, p)
    consts = (w["w_attn_out"], w["w_glu_val"], w["w_glu_gate"], w["w_out"], vec(w["g_mlp"]),
              w["w_up"], w["w_down"], vec(w["g_ple"]), w["w_ple"], w["w_ple_gate"])
    return pl.pallas_call(
        functools.partial(_post_kernel, ff_chunk=1024),
        grid=(m // tm,),
        in_specs=[row(a) for a in acts] + [_resident(c.shape) for c in consts],
        out_specs=pl.BlockSpec((tm, d), lambda i: (i, 0)),
        out_shape=jax.ShapeDtypeStruct((m, d), F32),
        compiler_params=_params(("parallel",)),
        name="post",
    )(*acts, *consts)


def kernel(x_prompt, x_sample, p_prompt, p_sample, cache_k, cache_v, state_ssm, page_table, g_mix, w_in, g_q, g_k, sb_bias, w_attn_out, ssm_a_re, ssm_a_im, ssm_log_dt, ssm_b_re, ssm_b_im, ssm_c_re, ssm_c_im, ssm_d, w_glu_val, w_glu_gate, w_out, g_mlp, w_up, w_down, g_ple, w_ple, w_ple_gate):
    depth = w_in.shape[0]
    batch, seq, d_model = x_prompt.shape
    dec_batch, dec_seq, _ = x_sample.shape
    m_p, m_s = batch * seq, dec_batch * dec_seq
    to_pages = lambda c: jnp.transpose(c, (0, 1, 3, 4, 2)).reshape(c.shape[:2] + (ATT_WIDTH, c.shape[2]))
    ck, cv = to_pages(cache_k), to_pages(cache_v)
    hd = jnp.arange(ATT_WIDTH, dtype=jnp.int32) // HEAD_DIM
    bd = jnp.where(hd[:, None] == hd[None, :], 1.0 / HEAD_DIM, 0.0).astype(BF16)

    mats = jax.vmap(lambda *p: _ssm_mats(*p, sorted({PROMPT_CHUNK, dec_seq})))(
        ssm_a_re, ssm_a_im, ssm_log_dt, ssm_b_re, ssm_b_im, ssm_c_re, ssm_c_im)

    xp, xs = x_prompt.reshape(m_p, d_model), x_sample.reshape(m_s, d_model)
    kt_all = vt_all = None
    outs = [[] for _ in range(4)]
    for i in range(depth):
        w_in_bf = w_in[i].astype(BF16)
        w = {
            "g_mix": g_mix[i], "w_in": w_in_bf, "bd": bd,
            "w_kvt": w_in_bf[:, ATT_WIDTH:3 * ATT_WIDTH].T,
            "g_q": jnp.tile(g_q[i], N_HEADS).reshape(1, ATT_WIDTH),
            "g_k": jnp.tile(g_k[i], N_HEADS).reshape(1, ATT_WIDTH),
            "w_attn_out": w_attn_out[i].astype(BF16), "w_glu_val": w_glu_val[i].astype(BF16),
            "w_glu_gate": w_glu_gate[i].astype(BF16), "w_out": w_out[i].astype(BF16),
            "g_mlp": g_mlp[i], "w_up": w_up[i].astype(BF16), "w_down": w_down[i].astype(BF16),
            "g_ple": g_ple[i], "w_ple": w_ple[i].astype(BF16), "w_ple_gate": w_ple_gate[i].astype(BF16),
        }
        bias = sb_bias[i].astype(F32)
        bias_rows = jnp.repeat(bias, dec_seq).reshape(N_HEADS * dec_seq, 1)

        q, kt_all, vt_all, u, ga, gs = _in_proj(xp, w, (i, depth, batch, seq, kt_all, vt_all))
        o = _attn_prompt(q, kt_all, vt_all, i, bias, batch, seq)
        y, h_p = _ssm(u, None, mats[PROMPT_CHUNK], i, ssm_d[i], batch, seq, PROMPT_CHUNK)
        xp = _post(xp, o, y, ga, gs, p_prompt[i].reshape(m_p, -1), w)

        q, k_s, v_s, u, ga, gs = _in_proj(xs, w)
        o = _attn_sample(q, k_s, v_s, ck, cv, i, page_table, bias_rows, dec_seq)
        y, h_s = _ssm(u, state_ssm[i], mats[dec_seq], i, ssm_d[i], dec_batch, dec_seq, dec_seq)
        xs = _post(xs, o, y, ga, gs, p_sample[i].reshape(m_s, -1), w)

        shape4 = (dec_batch, dec_seq, N_HEADS, HEAD_DIM)
        for lst, val in zip(outs, (h_p, k_s.reshape(shape4), v_s.reshape(shape4), h_s)):
            lst.append(val)
    s_p, k_s, v_s, s_s = (jnp.stack(o) for o in outs)
    from_t = lambda t: jnp.transpose(t.reshape(depth, batch, N_HEADS, HEAD_DIM, seq), (0, 1, 4, 2, 3))
    return (xp.reshape(x_prompt.shape), xs.reshape(x_sample.shape), from_t(kt_all), from_t(vt_all), s_p,
            k_s, v_s, s_s)
```

```python
import functools

import jax
import jax.numpy as jnp
from jax import lax
from jax.experimental import pallas as pl
from jax.experimental.pallas import tpu as pltpu

F32 = jnp.float32
BF16 = jnp.bfloat16

N_HEADS = 8
HEAD_DIM = 64
ATT_WIDTH = N_HEADS * HEAD_DIM
GROUP_CH = 16
N_GROUPS = 32
SSM_STATE = 64
D_SSM = N_GROUPS * GROUP_CH
RMS_EPS = 1e-6
LOG2E = 1.4426950408889634
Q_SCALE = -(HEAD_DIM ** -0.5) * LOG2E
LANES = 128
MXU_TILE = 256
GROUPS_PER_BLOCK = LANES // GROUP_CH
VMEM_LIMIT = 56 * 1024 * 1024

PROMPT_CHUNK = 16
SSM_ROW_CHUNK = 64
ATT_TK = 256
ATT_TQ = 2 * ATT_TK
ROW_TILE = 256


def _dot(a, b):
    return jnp.dot(a, b, preferred_element_type=F32)


def _dot_nt(a, b):
    return lax.dot_general(a, b, (((1,), (1,)), ((), ())), preferred_element_type=F32)


def _split(x):
    hi = x.astype(BF16)
    lo = (x - hi.astype(F32)).astype(BF16)
    return hi, lo


def _dot3(a, b):
    ah, al = _split(a)
    bh, bl = _split(b)
    return _dot(ah, bh) + _dot(al, bh) + _dot(ah, bl)


def _params(sem):
    return pltpu.CompilerParams(dimension_semantics=sem, vmem_limit_bytes=VMEM_LIMIT)


def _resident(shape):
    nd = len(shape)
    return pl.BlockSpec(shape, lambda *_: (0,) * nd, pipeline_mode=pl.Buffered(1))


def _in_proj_kernel(x_ref, gmix_ref, w_ref, bd_ref, gq_ref, gk_ref, *rest, kv_transposed, n_prev):
    x = x_ref[...]
    ms = jnp.mean(x * x, axis=-1, keepdims=True)
    h = (x * lax.rsqrt(ms + RMS_EPS) * gmix_ref[...]).astype(BF16)
    a = ATT_WIDTH

    def head_norm(p, g_ref):
        msh = _dot((p * p).astype(BF16), bd_ref[...])
        return p * lax.rsqrt(msh + RMS_EPS) * g_ref[...]

    if kv_transposed:
        wkvt_ref = rest[0]
        q_ref, k_ref, v_ref, u_ref, ga_ref, gs_ref = rest[1 + n_prev:]
        kvt = jnp.concatenate([_dot_nt(wkvt_ref[r:r + MXU_TILE, :], h) for r in range(0, 2 * a, MXU_TILE)], axis=0)
        kt = kvt[:a]
        mst = _dot(bd_ref[...], (kt * kt).astype(BF16))
        k_ref[...] = kt * lax.rsqrt(mst + RMS_EPS) * gk_ref[...]
        v_ref[...] = kvt[a:]
    else:
        q_ref, k_ref, v_ref, u_ref, ga_ref, gs_ref = rest
        k_ref[...] = head_norm(_dot(h, w_ref[:, a:2 * a]), gk_ref)
        v_ref[...] = _dot(h, w_ref[:, 2 * a:3 * a])
    q_ref[...] = (head_norm(_dot(h, w_ref[:, 0:a]), gq_ref) * Q_SCALE).astype(BF16)
    u_ref[...] = _dot(h, w_ref[:, 3 * a:3 * a + D_SSM])
    o = 3 * a + D_SSM
    d = ga_ref.shape[-1]
    ga_ref[...] = jax.nn.sigmoid(_dot(h, w_ref[:, o:o + d])).astype(BF16)
    gs_ref[...] = jax.nn.sigmoid(_dot(h, w_ref[:, o + d:o + 2 * d])).astype(BF16)


def _in_proj(x, w, kv_slot=None):
    m, d = x.shape
    tm = min(ROW_TILE, m)
    row = lambda width: pl.BlockSpec((tm, width), lambda i: (i, 0))
    tok = lambda width, t: (row(width), jax.ShapeDtypeStruct((m, width), t))
    ins = [x, w["g_mix"].reshape(1, d), w["w_in"], w["bd"], w["g_q"]]
    in_specs = [row(d), _resident((1, d)), _resident(w["w_in"].shape), _resident(w["bd"].shape),
                _resident((1, ATT_WIDTH))]
    aliases = {}
    n_prev = 0
    if kv_slot is None:
        ins.append(w["g_k"])
        in_specs.append(_resident((1, ATT_WIDTH)))
        kv = [tok(ATT_WIDTH, F32)] * 2
    else:
        layer, depth, batch, seq, prev_kt, prev_vt = kv_slot
        per_seq = seq // tm
        ins += [w["g_k"].reshape(ATT_WIDTH, 1), w["w_kvt"]]
        in_specs += [_resident((ATT_WIDTH, 1)), _resident(w["w_kvt"].shape)]
        if prev_kt is not None:
            n_prev = 2
            aliases = {len(ins): 1, len(ins) + 1: 2}
            ins += [prev_kt, prev_vt]
            in_specs += [pl.BlockSpec(memory_space=pl.ANY)] * 2
        slot = pl.BlockSpec((None, None, ATT_WIDTH, tm), lambda i: (layer, i // per_seq, 0, i % per_seq))
        kv = [(slot, jax.ShapeDtypeStruct((depth, batch, ATT_WIDTH, seq), F32))] * 2
    outs = [tok(ATT_WIDTH, BF16)] + kv + [tok(D_SSM, F32), tok(d, BF16), tok(d, BF16)]
    return pl.pallas_call(
        functools.partial(_in_proj_kernel, kv_transposed=kv_slot is not None, n_prev=n_prev),
        grid=(m // tm,),
        in_specs=in_specs,
        out_specs=[o[0] for o in outs],
        out_shape=[o[1] for o in outs],
        input_output_aliases=aliases,
        compiler_params=_params(("parallel",)),
        name="in_proj",
    )(*ins)


def _log_stay(nz, mask):
    m = jnp.minimum(nz, 0.0)
    ls = m - jnp.log(1.0 + jnp.exp2(m + (m - nz))) * LOG2E
    return ls if mask is None else jnp.where(mask, ls, 0.0)


def _prefix_operand(ls, copies):
    return ls.astype(BF16) if copies == 1 else jnp.concatenate(_split(ls), axis=1)


def _sb_weights(ls, nz, later, carry, mask):
    w = jnp.exp2((ls - nz) + (later + carry))
    if mask is not None:
        w = jnp.where(mask, w, 0.0)
    return w, carry + (later[:, 0:1] + ls[:, 0:1])


def _sb_block(nz, mask, tri, carry):
    ls = _log_stay(nz, mask)
    later = _dot(_prefix_operand(ls, tri.shape[0] // tri.shape[1]), tri)
    return _sb_weights(ls, nz, later, carry, mask)


def _later_keys(n, copies):
    assert copies in (1, 2)
    r = lax.broadcasted_iota(jnp.int32, (copies * n, n), 0)
    c = lax.broadcasted_iota(jnp.int32, (copies * n, n), 1)
    return jnp.where(jnp.where(r >= n, r - n, r) > c, 1.0, 0.0).astype(BF16)


def _attn_prompt_kernel(bias_ref, q_ref, kt_ref, vt_ref, o_ref, kb_s, vb_s, acc_ref, carry_ref,
                        *, tq, tk, seq):
    hp = pl.program_id(1)
    qi = pl.program_id(2)
    lane = lax.broadcasted_iota(jnp.int32, (1, LANES), 1)
    first_head = lane < HEAD_DIM

    @pl.when(qi == 0)
    def _stage_keys_values():
        for i in range(seq // tk):
            kb_s[i] = kt_ref[:, i * tk:(i + 1) * tk].astype(BF16)
            vb_s[i] = vt_ref[:, i * tk:(i + 1) * tk].astype(BF16)

    qs = q_ref[...]
    q2 = jnp.concatenate([jnp.where(first_head, qs, 0), jnp.where(first_head, 0, qs)], axis=0)
    nbias = jnp.concatenate([jnp.full((tq, 1), bias_ref[2 * hp + h] * (-LOG2E), F32) for h in range(2)], axis=0)
    tri = _later_keys(tk, 1)
    r = lax.broadcasted_iota(jnp.int32, (2 * tq, tk), 0)
    c = lax.broadcasted_iota(jnp.int32, (2 * tq, tk), 1)
    q_pos = jnp.where(r < tq, r, r - tq) + qi * tq

    def block(kb, mask, carry):
        nz = _dot(q2, kb_s[kb]) + nbias
        w, carry = _sb_block(nz, mask, tri, carry)
        pv = _dot_nt(w.astype(BF16), vb_s[kb])
        return jnp.where(first_head, pv[:tq], pv[tq:]), carry

    def run(blocks, masked, carry):
        total = None
        for kb in blocks:
            contrib, carry = block(kb, (c + kb * tk < q_pos) if masked else None, carry)
            total = contrib if total is None else total + contrib
        return total, carry

    per_q = tq // tk
    top = per_q * (qi + 1) - 1
    total, carry = run([top - d for d in range(per_q)], True, jnp.zeros((2 * tq, 1), F32))
    acc_ref[...] = total
    carry_ref[...] = carry

    def below_diagonal(i, _):
        first = top - per_q * (i + 1)
        total, carry = run([first - d for d in range(per_q)], False, carry_ref[...])
        acc_ref[...] += total
        carry_ref[...] = carry
        return 0
    lax.fori_loop(0, qi, below_diagonal, 0)
    o_ref[...] = acc_ref[...].astype(BF16)


def _attn_prompt(q, kt_all, vt_all, layer, bias, batch, seq):
    tq, tk = ATT_TQ, ATT_TK
    nq = seq // tq
    kern = functools.partial(_attn_prompt_kernel, tq=tq, tk=tk, seq=seq)
    head_pair = pl.BlockSpec((None, None, LANES, seq), lambda b, hp, i: (layer, b, hp, 0))
    return pl.pallas_call(
        kern,
        grid=(batch, N_HEADS // 2, nq),
        in_specs=[pl.BlockSpec(memory_space=pltpu.SMEM),
                  pl.BlockSpec((tq, LANES), lambda b, hp, i: (b * nq + i, hp)),
                  head_pair, head_pair],
        out_specs=pl.BlockSpec((tq, LANES), lambda b, hp, i: (b * nq + i, hp)),
        out_shape=jax.ShapeDtypeStruct(q.shape, BF16),
        scratch_shapes=[pltpu.VMEM((seq // tk, LANES, tk), BF16)] * 2
                       + [pltpu.VMEM((tq, LANES), F32), pltpu.VMEM((2 * tq, 1), F32)],
        compiler_params=_params(("parallel", "parallel", "arbitrary")),
        name="attn_prompt",
    )(bias, q, kt_all, vt_all)


def _attn_sample_kernel(pt_ref, bias_ref, hmask_ref, nmask_ref, q_ref, kn_ref, vn_ref, *rest,
                        n_pages, t_new, page):
    del pt_ref
    k_pages = rest[:n_pages]
    v_pages = rest[n_pages:2 * n_pages]
    o_ref = rest[2 * n_pages]
    rows = N_HEADS * t_new
    same_head = hmask_ref[...] > 0.5
    q = q_ref[...].astype(F32)
    q_bd = jnp.where(same_head, jnp.concatenate([q] * N_HEADS, axis=0), 0.0).astype(BF16)
    nbias = bias_ref[...] * (-LOG2E)

    pad = jnp.zeros((page - t_new, ATT_WIDTH), F32)
    k_new = jnp.concatenate([kn_ref[...], pad], axis=0).astype(BF16)
    v_new = jnp.concatenate([vn_ref[...], pad], axis=0).astype(BF16)
    new_mask = nmask_ref[...] > 0.5
    nz = _dot_nt(q_bd, k_new) + nbias
    w, carry = _sb_block(nz, new_mask, _later_keys(page, 2), jnp.zeros((rows, 1), F32))
    acc = _dot(w.astype(BF16), v_new)

    per_block = 2 if n_pages % 2 == 0 else 1
    bw = per_block * page
    n_blk = n_pages // per_block
    all_pages = lambda refs: jnp.concatenate([r[...] for r in refs], axis=1).astype(BF16)
    nz = _dot(q_bd, all_pages(k_pages)) + nbias
    ls = _log_stay(nz, None)
    blocks = [slice(jb * bw, (jb + 1) * bw) for jb in range(n_blk)]
    later = _dot(jnp.concatenate([_prefix_operand(ls[:, b], 2) for b in blocks], axis=0), _later_keys(bw, 2))
    ws = [None] * n_blk
    for jb in reversed(range(n_blk)):
        b = blocks[jb]
        w, carry = _sb_weights(ls[:, b], nz[:, b], later[jb * rows:(jb + 1) * rows], carry, None)
        ws[jb] = w.astype(BF16)
    acc = acc + _dot_nt(jnp.concatenate(ws, axis=1), all_pages(v_pages))
    acc = jnp.where(same_head, acc, 0.0)
    out = acc[0:t_new, :]
    for h in range(1, N_HEADS):
        out = out + acc[h * t_new:(h + 1) * t_new, :]
    o_ref[...] = out.astype(BF16)


def _attn_sample(q, k_new, v_new, cache_k, cache_v, layer, page_table, bias_rows, t_new):
    n_seq, n_pages = page_table.shape
    page = cache_k.shape[3]
    page_spec = lambda j: pl.BlockSpec((None, None, ATT_WIDTH, page),
                                       lambda b, pt, j=j: (layer, pt[b, j], 0, 0))
    tok = pl.BlockSpec((t_new, ATT_WIDTH), lambda b, pt: (b, 0))
    seq_blk = pl.BlockSpec((None, t_new, ATT_WIDTH), lambda b, pt: (b, 0, 0))
    rows = N_HEADS * t_new
    whole = lambda w: pl.BlockSpec((rows, w), lambda b, pt: (0, 0))
    row_head = jnp.arange(rows, dtype=jnp.int32)[:, None] // t_new
    row_query = jnp.arange(rows, dtype=jnp.int32)[:, None] % t_new
    head_mask = (row_head == jnp.arange(ATT_WIDTH, dtype=jnp.int32)[None, :] // HEAD_DIM).astype(F32)
    new_mask = (jnp.arange(page, dtype=jnp.int32)[None, :] < row_query).astype(F32)
    kern = functools.partial(_attn_sample_kernel, n_pages=n_pages, t_new=t_new, page=page)
    grid_spec = pltpu.PrefetchScalarGridSpec(
        num_scalar_prefetch=1,
        grid=(n_seq,),
        in_specs=[whole(1), whole(ATT_WIDTH), whole(page), seq_blk, tok, tok]
                 + [page_spec(j) for j in range(n_pages)] * 2,
        out_specs=seq_blk,
    )
    o = pl.pallas_call(
        kern,
        grid_spec=grid_spec,
        out_shape=jax.ShapeDtypeStruct((n_seq, t_new, ATT_WIDTH), BF16),
        compiler_params=_params(("arbitrary",)),
        name="attn_sample",
    )(page_table, bias_rows, head_mask, new_mask, q.reshape(n_seq, t_new, ATT_WIDTH), k_new, v_new,
      *([cache_k] * n_pages), *([cache_v] * n_pages))
    return o.reshape(q.shape)


def _ssm_kernel(u_ref, d_ref, m_ref, wre_ref, wim_ref, vre_ref, vim_ref, a_ref, h0re_ref, h0im_ref,
                y_ref, hre_ref, him_ref, ug_s, yg_s, zre_s, zim_s, inre_s, inim_s, *, L, n_steps, bp):
    n_rows = n_steps * bp
    p = SSM_STATE
    gpb = GROUPS_PER_BLOCK
    slot = lax.shift_right_logical(lax.broadcasted_iota(jnp.int32, (1, LANES), 1), GROUP_CH.bit_length() - 1)

    def transpose_slots(a):
        a = list(a)
        k = gpb // 2
        while k:
            upper = jnp.bitwise_and(slot, k) != 0
            for i in range(gpb):
                if not i & k:
                    lo, hi = a[i], a[i + k]
                    a[i] = jnp.where(upper, pltpu.roll(hi, k * GROUP_CH, 1), lo)
                    a[i + k] = jnp.where(upper, hi, pltpu.roll(lo, LANES - k * GROUP_CH, 1))
            k //= 2
        return a

    rc = min(n_rows, SSM_ROW_CHUNK)
    n_half = L // gpb
    for r0 in range(0, n_rows, rc):
        for h in range(n_half):
            steps = [u_ref[pl.ds(r0 * L + h * gpb + j, rc, stride=L), :] for j in range(gpb)]
            for g, u_gh in enumerate(transpose_slots(steps)):
                ug_s[g, r0:r0 + rc, h * LANES:(h + 1) * LANES] = u_gh
    for g in range(gpb):
        u_g = ug_s[g]
        zre_s[:, g * p:(g + 1) * p] = _dot3(u_g, wre_ref[g])
        zim_s[:, g * p:(g + 1) * p] = _dot3(u_g, wim_ref[g])

    ar = a_ref[0:1, :]
    ai = a_ref[1:2, :]

    def step(k, carry):
        hr, hi = carry
        rows = pl.ds(k * bp, bp)
        inre_s[rows, :] = hr
        inim_s[rows, :] = hi
        zr = zre_s[rows, :]
        zi = zim_s[rows, :]
        return ar * hr - ai * hi + zr, ar * hi + ai * hr + zi

    hr, hi = lax.fori_loop(0, n_steps, step, (h0re_ref[...], h0im_ref[...]))
    hre_ref[...] = hr
    him_ref[...] = hi

    bdot = lambda a, b: _dot(a.astype(BF16), b.astype(BF16))
    for g in range(gpb):
        yg_s[g] = (bdot(ug_s[g], m_ref[g]) + bdot(inre_s[:, g * p:(g + 1) * p], vre_ref[g])
                   + bdot(inim_s[:, g * p:(g + 1) * p], vim_ref[g]))
    for r0 in range(0, n_rows, rc):
        for h in range(n_half):
            groups = [yg_s[g, r0:r0 + rc, h * LANES:(h + 1) * LANES] for g in range(gpb)]
            for j, y_step in enumerate(transpose_slots(groups)):
                rows = pl.ds(r0 * L + h * gpb + j, rc, stride=L)
                y_ref[rows, :] = y_step + d_ref[...] * u_ref[rows, :]


def _ssm_mats(a_re, a_im, log_dt, b_re, b_im, c_re, c_im, lengths):
    hp = lax.Precision.HIGHEST
    L = max(lengths)
    dt = jnp.exp(log_dt)[:, None]
    mag = jnp.exp(a_re * dt)
    ab_re, ab_im = mag * jnp.cos(a_im * dt), mag * jnp.sin(a_im * dt)
    den = a_re * a_re + a_im * a_im
    nr, ni = ab_re - 1.0, ab_im
    f_re = (nr * a_re + ni * a_im) / den
    f_im = (ni * a_re - nr * a_im) / den
    bb_re = f_re[..., None] * b_re - f_im[..., None] * b_im
    bb_im = f_re[..., None] * b_im + f_im[..., None] * b_re
    pr, pi = [jnp.ones_like(ab_re)], [jnp.zeros_like(ab_re)]
    for _ in range(L):
        pr, pi = pr + [ab_re * pr[-1] - ab_im * pi[-1]], pi + [ab_re * pi[-1] + ab_im * pr[-1]]
    p_re, p_im = jnp.stack(pr), jnp.stack(pi)
    abr = p_re[:L, :, :, None] * bb_re - p_im[:L, :, :, None] * bb_im
    abi = p_re[:L, :, :, None] * bb_im + p_im[:L, :, :, None] * bb_re
    kern = (jnp.einsum('gcp,tgpd->tgcd', c_re, abr, precision=hp)
            - jnp.einsum('gcp,tgpd->tgcd', c_im, abi, precision=hp))
    idx = jnp.arange(L)
    delay = (idx[None, None, :] - idx[None, :, None] == idx[:, None, None]).astype(F32)
    flip = (idx[:, None] + idx[None, :] == L - 1).astype(F32)
    lc = L * GROUP_CH
    m = jnp.einsum('zst,zgcd->gsdtc', delay, kern, precision=hp).reshape(N_GROUPS, lc, lc)
    w_re = jnp.einsum('sz,zgpd->gsdp', flip, abr, precision=hp).reshape(N_GROUPS, lc, SSM_STATE)
    w_im = jnp.einsum('sz,zgpd->gsdp', flip, abi, precision=hp).reshape(N_GROUPS, lc, SSM_STATE)
    cr = jnp.transpose(c_re, (0, 2, 1))[:, :, None, :]
    ci = jnp.transpose(c_im, (0, 2, 1))[:, :, None, :]
    qr = jnp.transpose(p_re[1:], (1, 2, 0))[..., None]
    qi = jnp.transpose(p_im[1:], (1, 2, 0))[..., None]
    v_re = (cr * qr - ci * qi).reshape(N_GROUPS, SSM_STATE, lc)
    v_im = (-cr * qi - ci * qr).reshape(N_GROUPS, SSM_STATE, lc)

    def for_length(n):
        nc, skip = n * GROUP_CH, (L - n) * GROUP_CH
        a_n = jnp.stack([p_re[n], p_im[n]], axis=1)
        return (m[:, :nc, :nc], w_re[:, skip:], w_im[:, skip:], v_re[:, :, :nc], v_im[:, :, :nc], a_n)

    return {n: for_length(n) for n in lengths}


def _ssm(u, h0, mats, layer, d_skip, batch, seq, L):
    m, w_re, w_im, v_re, v_im, a_l = mats
    depth = m.shape[0]
    gpb = GROUPS_PER_BLOCK
    n_cb = N_GROUPS // gpb
    lc = L * GROUP_CH
    state_w = gpb * SSM_STATE
    if seq == L:
        n_bblk, bp, n_steps = 1, batch, 1
    else:
        n_bblk, bp, n_steps = batch, 1, seq // L
    tok = bp * n_steps * L
    n_rows = bp * n_steps

    def to_blocks(x):
        x = x.reshape(n_bblk, bp, n_cb, state_w)
        return jnp.transpose(x, (0, 2, 1, 3))

    if h0 is None:
        h0re = h0im = jnp.zeros((n_bblk, n_cb, bp, state_w), F32)
    else:
        h0re, h0im = to_blocks(h0[..., 0]), to_blocks(h0[..., 1])
    a_blk = jnp.transpose(a_l.reshape(depth, n_cb, gpb, 2, SSM_STATE), (0, 1, 3, 2, 4))
    a_blk = a_blk.reshape(depth, n_cb, 2, state_w)

    grp = lambda a, b: pl.BlockSpec((None, gpb, a, b), lambda i, cb: (layer, cb, 0, 0))
    act = pl.BlockSpec((tok, LANES), lambda i, cb: (i, cb))
    state = pl.BlockSpec((None, None, bp, state_w), lambda i, cb: (i, cb, 0, 0))
    state_shape = jax.ShapeDtypeStruct((n_bblk, n_cb, bp, state_w), F32)
    kern = functools.partial(_ssm_kernel, L=L, n_steps=n_steps, bp=bp)
    y, hre, him = pl.pallas_call(
        kern,
        grid=(n_bblk, n_cb),
        in_specs=[act, pl.BlockSpec((1, LANES), lambda i, cb: (0, cb)),
                  grp(lc, lc), grp(lc, SSM_STATE), grp(lc, SSM_STATE),
                  grp(SSM_STATE, lc), grp(SSM_STATE, lc),
                  pl.BlockSpec((None, None, 2, state_w), lambda i, cb: (layer, cb, 0, 0)), state, state],
        out_specs=[act, state, state],
        out_shape=[jax.ShapeDtypeStruct(u.shape, F32), state_shape, state_shape],
        scratch_shapes=[pltpu.VMEM((gpb, n_rows, lc), F32)] * 2 + [pltpu.VMEM((n_rows, state_w), F32)] * 4,
        compiler_params=_params(("parallel", "parallel")),
        name="ssm",
    )(u, d_skip.reshape(1, D_SSM), m, w_re, w_im, v_re, v_im, a_blk, h0re, h0im)

    def from_blocks(x):
        return jnp.transpose(x, (0, 2, 1, 3)).reshape(batch, N_GROUPS, SSM_STATE)

    return y, jnp.stack([from_blocks(hre), from_blocks(him)], axis=-1)


def _post_kernel(x_ref, o_ref, y_ref, ga_ref, gs_ref, p_ref,
                 wao_ref, wgv_ref, wgg_ref, wout_ref, gmlp_ref, wup_ref, wdown_ref,
                 gple_ref, wple_ref, wpg_ref, out_ref, *, ff_chunk):
    def rms(x, g_ref):
        ms = jnp.mean(x * x, axis=-1, keepdims=True)
        return (x * lax.rsqrt(ms + RMS_EPS) * g_ref[...]).astype(BF16)

    y_act = jax.nn.gelu(y_ref[...], approximate=True).astype(BF16)
    attn_out = _dot(o_ref[...], wao_ref[...])
    ssm_out = _dot(y_act, wgv_ref[...]) * jax.nn.sigmoid(_dot(y_act, wgg_ref[...]))
    merged = ga_ref[...].astype(F32) * attn_out + gs_ref[...].astype(F32) * ssm_out
    x = x_ref[...] + _dot(merged.astype(BF16), wout_ref[...])
    h2 = rms(x, gmlp_ref)
    d_ff = wup_ref.shape[1]
    mlp = None
    for c0 in range(0, d_ff, ff_chunk):
        hid = jnp.square(jnp.maximum(_dot(h2, wup_ref[:, c0:c0 + ff_chunk]), 0.0)).astype(BF16)
        part = _dot(hid, wdown_ref[c0:c0 + ff_chunk, :])
        mlp = part if mlp is None else mlp + part
    x = x + mlp
    h3 = rms(x, gple_ref)
    out_ref[...] = x + _dot(p_ref[...].astype(BF16), wple_ref[...]) * jax.nn.sigmoid(_dot(h3, wpg_ref[...]))


def _post(x, o, y, ga, gs, p, w):
    m, d = x.shape
    tm = min(ROW_TILE, m)
    row = lambda a: pl.BlockSpec((tm, a.shape[1]), lambda i: (i, 0))
    vec = lambda a: a.reshape(1, -1)
    acts = (x, o, y, ga, gs, p)
    consts = (w["w_attn_out"], w["w_glu_val"], w["w_glu_gate"], w["w_out"], vec(w["g_mlp"]),
              w["w_up"], w["w_down"], vec(w["g_ple"]), w["w_ple"], w["w_ple_gate"])
    return pl.pallas_call(
        functools.partial(_post_kernel, ff_chunk=1024),
        grid=(m // tm,),
        in_specs=[row(a) for a in acts] + [_resident(c.shape) for c in consts],
        out_specs=pl.BlockSpec((tm, d), lambda i: (i, 0)),
        out_shape=jax.ShapeDtypeStruct((m, d), F32),
        compiler_params=_params(("parallel",)),
        name="post",
    )(*acts, *consts)


def kernel(x_prompt, x_sample, p_prompt, p_sample, cache_k, cache_v, state_ssm, page_table, g_mix, w_in, g_q, g_k, sb_bias, w_attn_out, ssm_a_re, ssm_a_im, ssm_log_dt, ssm_b_re, ssm_b_im, ssm_c_re, ssm_c_im, ssm_d, w_glu_val, w_glu_gate, w_out, g_mlp, w_up, w_down, g_ple, w_ple, w_ple_gate):
    depth = w_in.shape[0]
    batch, seq, d_model = x_prompt.shape
    dec_batch, dec_seq, _ = x_sample.shape
    m_p, m_s = batch * seq, dec_batch * dec_seq
    to_pages = lambda c: jnp.transpose(c, (0, 1, 3, 4, 2)).reshape(c.shape[:2] + (ATT_WIDTH, c.shape[2]))
    ck, cv = to_pages(cache_k), to_pages(cache_v)
    hd = jnp.arange(ATT_WIDTH, dtype=jnp.int32) // HEAD_DIM
    bd = jnp.where(hd[:, None] == hd[None, :], 1.0 / HEAD_DIM, 0.0).astype(BF16)

    mats = jax.vmap(lambda *p: _ssm_mats(*p, sorted({PROMPT_CHUNK, dec_seq})))(
        ssm_a_re, ssm_a_im, ssm_log_dt, ssm_b_re, ssm_b_im, ssm_c_re, ssm_c_im)

    xp, xs = x_prompt.reshape(m_p, d_model), x_sample.reshape(m_s, d_model)
    kt_all = vt_all = None
    outs = [[] for _ in range(4)]
    for i in range(depth):
        w_in_bf = w_in[i].astype(BF16)
        w = {
            "g_mix": g_mix[i], "w_in": w_in_bf, "bd": bd,
            "w_kvt": w_in_bf[:, ATT_WIDTH:3 * ATT_WIDTH].T,
            "g_q": jnp.tile(g_q[i], N_HEADS).reshape(1, ATT_WIDTH),
            "g_k": jnp.tile(g_k[i], N_HEADS).reshape(1, ATT_WIDTH),
            "w_attn_out": w_attn_out[i].astype(BF16), "w_glu_val": w_glu_val[i].astype(BF16),
            "w_glu_gate": w_glu_gate[i].astype(BF16), "w_out": w_out[i].astype(BF16),
            "g_mlp": g_mlp[i], "w_up": w_up[i].astype(BF16), "w_down": w_down[i].astype(BF16),
            "g_ple": g_ple[i], "w_ple": w_ple[i].astype(BF16), "w_ple_gate": w_ple_gate[i].astype(BF16),
        }
        bias = sb_bias[i].astype(F32)
        bias_rows = jnp.repeat(bias, dec_seq).reshape(N_HEADS * dec_seq, 1)

        q, kt_all, vt_all, u, ga, gs = _in_proj(xp, w, (i, depth, batch, seq, kt_all, vt_all))
        o = _attn_prompt(q, kt_all, vt_all, i, bias, batch, seq)
        y, h_p = _ssm(u, None, mats[PROMPT_CHUNK], i, ssm_d[i], batch, seq, PROMPT_CHUNK)
        xp = _post(xp, o, y, ga, gs, p_prompt[i].reshape(m_p, -1), w)

        q, k_s, v_s, u, ga, gs = _in_proj(xs, w)
        o = _attn_sample(q, k_s, v_s, ck, cv, i, page_table, bias_rows, dec_seq)
        y, h_s = _ssm(u, state_ssm[i], mats[dec_seq], i, ssm_d[i], dec_batch, dec_seq, dec_seq)
        xs = _post(xs, o, y, ga, gs, p_sample[i].reshape(m_s, -1), w)

        shape4 = (dec_batch, dec_seq, N_HEADS, HEAD_DIM)
        for lst, val in zip(outs, (h_p, k_s.reshape(shape4), v_s.reshape(shape4), h_s)):
            lst.append(val)
    s_p, k_s, v_s, s_s = (jnp.stack(o) for o in outs)
    from_t = lambda t: jnp.transpose(t.reshape(depth, batch, N_HEADS, HEAD_DIM, seq), (0, 1, 4, 2, 3))
    return (xp.reshape(x_prompt.shape), xs.reshape(x_sample.shape), from_t(kt_all), from_t(vt_all), s_p,
            k_s, v_s, s_s)
```

```python
import functools

import jax
import jax.numpy as jnp
from jax import lax
from jax.experimental import pallas as pl
from jax.experimental.pallas import tpu as pltpu

F32 = jnp.float32
BF16 = jnp.bfloat16

N_HEADS = 8
HEAD_DIM = 64
ATT_WIDTH = N_HEADS * HEAD_DIM
GROUP_CH = 16
N_GROUPS = 32
SSM_STATE = 64
D_SSM = N_GROUPS * GROUP_CH
RMS_EPS = 1e-6
LOG2E = 1.4426950408889634
Q_SCALE = -(HEAD_DIM ** -0.5) * LOG2E
LANES = 128
MXU_TILE = 256
GROUPS_PER_BLOCK = LANES // GROUP_CH
VMEM_LIMIT = 56 * 1024 * 1024

PROMPT_CHUNK = 16
SSM_ROW_CHUNK = 64
ATT_TK = 256
ATT_TQ = 2 * ATT_TK
ROW_TILE = 256


def _dot(a, b):
    return jnp.dot(a, b, preferred_element_type=F32)


def _dot_nt(a, b):
    return lax.dot_general(a, b, (((1,), (1,)), ((), ())), preferred_element_type=F32)


def _split(x):
    hi = x.astype(BF16)
    lo = (x - hi.astype(F32)).astype(BF16)
    return hi, lo


def _dot3(a, b):
    ah, al = _split(a)
    bh, bl = _split(b)
    return _dot(ah, bh) + _dot(al, bh) + _dot(ah, bl)


def _params(sem):
    return pltpu.CompilerParams(dimension_semantics=sem, vmem_limit_bytes=VMEM_LIMIT)


def _resident(shape):
    nd = len(shape)
    return pl.BlockSpec(shape, lambda *_: (0,) * nd, pipeline_mode=pl.Buffered(1))


def _in_proj_kernel(x_ref, gmix_ref, w_ref, bd_ref, gq_ref, gk_ref, *rest, kv_transposed, n_prev):
    x = x_ref[...]
    ms = jnp.mean(x * x, axis=-1, keepdims=True)
    h = (x * lax.rsqrt(ms + RMS_EPS) * gmix_ref[...]).astype(BF16)
    a = ATT_WIDTH

    def head_norm(p, g_ref):
        msh = _dot((p * p).astype(BF16), bd_ref[...])
        return p * lax.rsqrt(msh + RMS_EPS) * g_ref[...]

    if kv_transposed:
        wkvt_ref = rest[0]
        q_ref, k_ref, v_ref, u_ref, ga_ref, gs_ref = rest[1 + n_prev:]
        kvt = jnp.concatenate([_dot_nt(wkvt_ref[r:r + MXU_TILE, :], h) for r in range(0, 2 * a, MXU_TILE)], axis=0)
        kt = kvt[:a]
        mst = _dot(bd_ref[...], (kt * kt).astype(BF16))
        k_ref[...] = kt * lax.rsqrt(mst + RMS_EPS) * gk_ref[...]
        v_ref[...] = kvt[a:]
    else:
        q_ref, k_ref, v_ref, u_ref, ga_ref, gs_ref = rest
        k_ref[...] = head_norm(_dot(h, w_ref[:, a:2 * a]), gk_ref)
        v_ref[...] = _dot(h, w_ref[:, 2 * a:3 * a])
    q_ref[...] = (head_norm(_dot(h, w_ref[:, 0:a]), gq_ref) * Q_SCALE).astype(BF16)
    u_ref[...] = _dot(h, w_ref[:, 3 * a:3 * a + D_SSM])
    o = 3 * a + D_SSM
    d = ga_ref.shape[-1]
    ga_ref[...] = jax.nn.sigmoid(_dot(h, w_ref[:, o:o + d])).astype(BF16)
    gs_ref[...] = jax.nn.sigmoid(_dot(h, w_ref[:, o + d:o + 2 * d])).astype(BF16)


def _in_proj(x, w, kv_slot=None):
    m, d = x.shape
    tm = min(ROW_TILE, m)
    row = lambda width: pl.BlockSpec((tm, width), lambda i: (i, 0))
    tok = lambda width, t: (row(width), jax.ShapeDtypeStruct((m, width), t))
    ins = [x, w["g_mix"].reshape(1, d), w["w_in"], w["bd"], w["g_q"]]
    in_specs = [row(d), _resident((1, d)), _resident(w["w_in"].shape), _resident(w["bd"].shape),
                _resident((1, ATT_WIDTH))]
    aliases = {}
    n_prev = 0
    if kv_slot is None:
        ins.append(w["g_k"])
        in_specs.append(_resident((1, ATT_WIDTH)))
        kv = [tok(ATT_WIDTH, F32)] * 2
    else:
        layer, depth, batch, seq, prev_kt, prev_vt = kv_slot
        per_seq = seq // tm
        ins += [w["g_k"].reshape(ATT_WIDTH, 1), w["w_kvt"]]
        in_specs += [_resident((ATT_WIDTH, 1)), _resident(w["w_kvt"].shape)]
        if prev_kt is not None:
            n_prev = 2
            aliases = {len(ins): 1, len(ins) + 1: 2}
            ins += [prev_kt, prev_vt]
            in_specs += [pl.BlockSpec(memory_space=pl.ANY)] * 2
        slot = pl.BlockSpec((None, None, ATT_WIDTH, tm), lambda i: (layer, i // per_seq, 0, i % per_seq))
        kv = [(slot, jax.ShapeDtypeStruct((depth, batch, ATT_WIDTH, seq), F32))] * 2
    outs = [tok(ATT_WIDTH, BF16)] + kv + [tok(D_SSM, F32), tok(d, BF16), tok(d, BF16)]
    return pl.pallas_call(
        functools.partial(_in_proj_kernel, kv_transposed=kv_slot is not None, n_prev=n_prev),
        grid=(m // tm,),
        in_specs=in_specs,
        out_specs=[o[0] for o in outs],
        out_shape=[o[1] for o in outs],
        input_output_aliases=aliases,
        compiler_params=_params(("parallel",)),
        name="in_proj",
    )(*ins)


def _log_stay(nz, mask):
    m = jnp.minimum(nz, 0.0)
    ls = m - jnp.log(1.0 + jnp.exp2(m + (m - nz))) * LOG2E
    return ls if mask is None else jnp.where(mask, ls, 0.0)


def _prefix_operand(ls, copies):
    return ls.astype(BF16) if copies == 1 else jnp.concatenate(_split(ls), axis=1)


def _sb_weights(ls, nz, later, carry, mask):
    w = jnp.exp2((ls - nz) + (later + carry))
    if mask is not None:
        w = jnp.where(mask, w, 0.0)
    return w, carry + (later[:, 0:1] + ls[:, 0:1])


def _sb_block(nz, mask, tri, carry):
    ls = _log_stay(nz, mask)
    later = _dot(_prefix_operand(ls, tri.shape[0] // tri.shape[1]), tri)
    return _sb_weights(ls, nz, later, carry, mask)


def _later_keys(n, copies):
    assert copies in (1, 2)
    r = lax.broadcasted_iota(jnp.int32, (copies * n, n), 0)
    c = lax.broadcasted_iota(jnp.int32, (copies * n, n), 1)
    return jnp.where(jnp.where(r >= n, r - n, r) > c, 1.0, 0.0).astype(BF16)


def _attn_prompt_kernel(bias_ref, q_ref, kt_ref, vt_ref, o_ref, kb_s, vb_s, acc_ref, carry_ref,
                        *, tq, tk, seq):
    hp = pl.program_id(1)
    qi = pl.program_id(2)
    lane = lax.broadcasted_iota(jnp.int32, (1, LANES), 1)
    first_head = lane < HEAD_DIM

    @pl.when(qi == 0)
    def _stage_keys_values():
        for i in range(seq // tk):
            kb_s[i] = kt_ref[:, i * tk:(i + 1) * tk].astype(BF16)
            vb_s[i] = vt_ref[:, i * tk:(i + 1) * tk].astype(BF16)

    qs = q_ref[...]
    q2 = jnp.concatenate([jnp.where(first_head, qs, 0), jnp.where(first_head, 0, qs)], axis=0)
    nbias = jnp.concatenate([jnp.full((tq, 1), bias_ref[2 * hp + h] * (-LOG2E), F32) for h in range(2)], axis=0)
    tri = _later_keys(tk, 1)

    def block(kb, mask, carry, q_rows, nb_rows):
        n = q_rows.shape[0] // 2
        nz = _dot(q_rows, kb_s[kb]) + nb_rows
        w, carry = _sb_block(nz, mask, tri, carry)
        pv = _dot_nt(w.astype(BF16), vb_s[kb])
        return jnp.where(first_head, pv[:n], pv[n:]), carry

    assert tq == 2 * tk
    top = 2 * qi + 1
    r = lax.broadcasted_iota(jnp.int32, (2 * tk, tk), 0)
    c = lax.broadcasted_iota(jnp.int32, (2 * tk, tk), 1)
    triangle = c < jnp.where(r < tk, r, r - tk)
    half = lambda x, lo: jnp.concatenate([x[lo:lo + tk], x[tq + lo:tq + lo + tk]], axis=0)
    q_early, nb_early, q_late, nb_late = half(q2, 0), half(nbias, 0), half(q2, tk), half(nbias, tk)
    zero = jnp.zeros((2 * tk, 1), F32)
    late_top, c_late = block(top, triangle, zero, q_late, nb_late)
    late_full, c_late = block(top - 1, None, c_late, q_late, nb_late)
    early, c_early = block(top - 1, triangle, zero, q_early, nb_early)
    acc_ref[0:tk, :] = early
    acc_ref[tk:tq, :] = late_top + late_full
    carry_ref[...] = jnp.concatenate([c_early[:tk], c_late[:tk], c_early[tk:], c_late[tk:]], axis=0)

    def below(first, n):
        carry = carry_ref[...]
        total = None
        for d in range(n):
            contrib, carry = block(first - d, None, carry, q2, nbias)
            total = contrib if total is None else total + contrib
        acc_ref[...] += total
        carry_ref[...] = carry

    odd = jnp.bitwise_and(qi, 1)

    @pl.when(odd == 1)
    def _pair():
        below(top - 2, 2)

    def four(i, _):
        below(top - 2 - 2 * odd - 4 * i, 4)
        return 0
    lax.fori_loop(0, lax.shift_right_logical(qi, 1), four, 0)
    o_ref[...] = acc_ref[...].astype(BF16)


def _attn_prompt(q, kt_all, vt_all, layer, bias, batch, seq):
    tq, tk = ATT_TQ, ATT_TK
    nq = seq // tq
    kern = functools.partial(_attn_prompt_kernel, tq=tq, tk=tk, seq=seq)
    head_pair = pl.BlockSpec((None, None, LANES, seq), lambda b, hp, i: (layer, b, hp, 0))
    return pl.pallas_call(
        kern,
        grid=(batch, N_HEADS // 2, nq),
        in_specs=[pl.BlockSpec(memory_space=pltpu.SMEM),
                  pl.BlockSpec((tq, LANES), lambda b, hp, i: (b * nq + i, hp)),
                  head_pair, head_pair],
        out_specs=pl.BlockSpec((tq, LANES), lambda b, hp, i: (b * nq + i, hp)),
        out_shape=jax.ShapeDtypeStruct(q.shape, BF16),
        scratch_shapes=[pltpu.VMEM((seq // tk, LANES, tk), BF16)] * 2
                       + [pltpu.VMEM((tq, LANES), F32), pltpu.VMEM((2 * tq, 1), F32)],
        compiler_params=_params(("parallel", "parallel", "arbitrary")),
        name="attn_prompt",
    )(bias, q, kt_all, vt_all)


def _attn_sample_kernel(pt_ref, bias_ref, hmask_ref, nmask_ref, q_ref, kn_ref, vn_ref, *rest,
                        n_pages, t_new, page):
    del pt_ref
    k_pages = rest[:n_pages]
    v_pages = rest[n_pages:2 * n_pages]
    o_ref = rest[2 * n_pages]
    rows = N_HEADS * t_new
    same_head = hmask_ref[...] > 0.5
    q = q_ref[...].astype(F32)
    q_bd = jnp.where(same_head, jnp.concatenate([q] * N_HEADS, axis=0), 0.0).astype(BF16)
    nbias = bias_ref[...] * (-LOG2E)

    pad = jnp.zeros((page - t_new, ATT_WIDTH), F32)
    k_new = jnp.concatenate([kn_ref[...], pad], axis=0).astype(BF16)
    v_new = jnp.concatenate([vn_ref[...], pad], axis=0).astype(BF16)
    new_mask = nmask_ref[...] > 0.5
    nz = _dot_nt(q_bd, k_new) + nbias
    w, carry = _sb_block(nz, new_mask, _later_keys(page, 2), jnp.zeros((rows, 1), F32))
    acc = _dot(w.astype(BF16), v_new)

    per_block = 2 if n_pages % 2 == 0 else 1
    bw = per_block * page
    n_blk = n_pages // per_block
    all_pages = lambda refs: jnp.concatenate([r[...] for r in refs], axis=1).astype(BF16)
    nz = _dot(q_bd, all_pages(k_pages)) + nbias
    ls = _log_stay(nz, None)
    blocks = [slice(jb * bw, (jb + 1) * bw) for jb in range(n_blk)]
    later = _dot(jnp.concatenate([_prefix_operand(ls[:, b], 2) for b in blocks], axis=0), _later_keys(bw, 2))
    ws = [None] * n_blk
    for jb in reversed(range(n_blk)):
        b = blocks[jb]
        w, carry = _sb_weights(ls[:, b], nz[:, b], later[jb * rows:(jb + 1) * rows], carry, None)
        ws[jb] = w.astype(BF16)
    acc = acc + _dot_nt(jnp.concatenate(ws, axis=1), all_pages(v_pages))
    acc = jnp.where(same_head, acc, 0.0)
    out = acc[0:t_new, :]
    for h in range(1, N_HEADS):
        out = out + acc[h * t_new:(h + 1) * t_new, :]
    o_ref[...] = out.astype(BF16)


def _attn_sample(q, k_new, v_new, cache_k, cache_v, layer, page_table, bias_rows, t_new):
    n_seq, n_pages = page_table.shape
    page = cache_k.shape[3]
    page_spec = lambda j: pl.BlockSpec((None, None, ATT_WIDTH, page),
                                       lambda b, pt, j=j: (layer, pt[b, j], 0, 0))
    tok = pl.BlockSpec((t_new, ATT_WIDTH), lambda b, pt: (b, 0))
    seq_blk = pl.BlockSpec((None, t_new, ATT_WIDTH), lambda b, pt: (b, 0, 0))
    rows = N_HEADS * t_new
    whole = lambda w: pl.BlockSpec((rows, w), lambda b, pt: (0, 0))
    row_head = jnp.arange(rows, dtype=jnp.int32)[:, None] // t_new
    row_query = jnp.arange(rows, dtype=jnp.int32)[:, None] % t_new
    head_mask = (row_head == jnp.arange(ATT_WIDTH, dtype=jnp.int32)[None, :] // HEAD_DIM).astype(F32)
    new_mask = (jnp.arange(page, dtype=jnp.int32)[None, :] < row_query).astype(F32)
    kern = functools.partial(_attn_sample_kernel, n_pages=n_pages, t_new=t_new, page=page)
    grid_spec = pltpu.PrefetchScalarGridSpec(
        num_scalar_prefetch=1,
        grid=(n_seq,),
        in_specs=[whole(1), whole(ATT_WIDTH), whole(page), seq_blk, tok, tok]
                 + [page_spec(j) for j in range(n_pages)] * 2,
        out_specs=seq_blk,
    )
    o = pl.pallas_call(
        kern,
        grid_spec=grid_spec,
        out_shape=jax.ShapeDtypeStruct((n_seq, t_new, ATT_WIDTH), BF16),
        compiler_params=_params(("arbitrary",)),
        name="attn_sample",
    )(page_table, bias_rows, head_mask, new_mask, q.reshape(n_seq, t_new, ATT_WIDTH), k_new, v_new,
      *([cache_k] * n_pages), *([cache_v] * n_pages))
    return o.reshape(q.shape)


def _ssm_kernel(u_ref, d_ref, m_ref, wre_ref, wim_ref, vre_ref, vim_ref, a_ref, h0re_ref, h0im_ref,
                y_ref, hre_ref, him_ref, ug_s, yg_s, zre_s, zim_s, inre_s, inim_s, *, L, n_steps, bp):
    n_rows = n_steps * bp
    p = SSM_STATE
    gpb = GROUPS_PER_BLOCK
    slot = lax.shift_right_logical(lax.broadcasted_iota(jnp.int32, (1, LANES), 1), GROUP_CH.bit_length() - 1)

    def transpose_slots(a):
        a = list(a)
        k = gpb // 2
        while k:
            upper = jnp.bitwise_and(slot, k) != 0
            for i in range(gpb):
                if not i & k:
                    lo, hi = a[i], a[i + k]
                    a[i] = jnp.where(upper, pltpu.roll(hi, k * GROUP_CH, 1), lo)
                    a[i + k] = jnp.where(upper, hi, pltpu.roll(lo, LANES - k * GROUP_CH, 1))
            k //= 2
        return a

    rc = min(n_rows, SSM_ROW_CHUNK)
    assert n_rows % rc == 0 and L % gpb == 0
    n_half = L // gpb
    for r0 in range(0, n_rows, rc):
        for h in range(n_half):
            steps = [u_ref[pl.ds(r0 * L + h * gpb + j, rc, stride=L), :] for j in range(gpb)]
            for g, u_gh in enumerate(transpose_slots(steps)):
                ug_s[g, r0:r0 + rc, h * LANES:(h + 1) * LANES] = u_gh
    for g in range(gpb):
        u_g = ug_s[g]
        zre_s[:, g * p:(g + 1) * p] = _dot3(u_g, wre_ref[g])
        zim_s[:, g * p:(g + 1) * p] = _dot3(u_g, wim_ref[g])

    ar = a_ref[0:1, :]
    ai = a_ref[1:2, :]

    def step(k, carry):
        hr, hi = carry
        rows = pl.ds(k * bp, bp)
        inre_s[rows, :] = hr
        inim_s[rows, :] = hi
        zr = zre_s[rows, :]
        zi = zim_s[rows, :]
        return ar * hr - ai * hi + zr, ar * hi + ai * hr + zi

    hr, hi = lax.fori_loop(0, n_steps, step, (h0re_ref[...], h0im_ref[...]))
    hre_ref[...] = hr
    him_ref[...] = hi

    bdot = lambda a, b: _dot(a.astype(BF16), b.astype(BF16))
    for g in range(gpb):
        yg_s[g] = (bdot(ug_s[g], m_ref[g]) + bdot(inre_s[:, g * p:(g + 1) * p], vre_ref[g])
                   + bdot(inim_s[:, g * p:(g + 1) * p], vim_ref[g]))
    for r0 in range(0, n_rows, rc):
        for h in range(n_half):
            groups = [yg_s[g, r0:r0 + rc, h * LANES:(h + 1) * LANES] for g in range(gpb)]
            for j, y_step in enumerate(transpose_slots(groups)):
                rows = pl.ds(r0 * L + h * gpb + j, rc, stride=L)
                y_ref[rows, :] = y_step + d_ref[...] * u_ref[rows, :]


def _ssm_mats(a_re, a_im, log_dt, b_re, b_im, c_re, c_im, lengths):
    hp = lax.Precision.HIGHEST
    L = max(lengths)
    dt = jnp.exp(log_dt)[:, None]
    mag = jnp.exp(a_re * dt)
    ab_re, ab_im = mag * jnp.cos(a_im * dt), mag * jnp.sin(a_im * dt)
    den = a_re * a_re + a_im * a_im
    nr, ni = ab_re - 1.0, ab_im
    f_re = (nr * a_re + ni * a_im) / den
    f_im = (ni * a_re - nr * a_im) / den
    bb_re = f_re[..., None] * b_re - f_im[..., None] * b_im
    bb_im = f_re[..., None] * b_im + f_im[..., None] * b_re
    pr, pi = [jnp.ones_like(ab_re)], [jnp.zeros_like(ab_re)]
    for _ in range(L):
        pr, pi = pr + [ab_re * pr[-1] - ab_im * pi[-1]], pi + [ab_re * pi[-1] + ab_im * pr[-1]]
    p_re, p_im = jnp.stack(pr), jnp.stack(pi)
    abr = p_re[:L, :, :, None] * bb_re - p_im[:L, :, :, None] * bb_im
    abi = p_re[:L, :, :, None] * bb_im + p_im[:L, :, :, None] * bb_re
    kern = (jnp.einsum('gcp,tgpd->tgcd', c_re, abr, precision=hp)
            - jnp.einsum('gcp,tgpd->tgcd', c_im, abi, precision=hp))
    lc = L * GROUP_CH
    resp = jnp.transpose(kern, (1, 3, 0, 2)).reshape(N_GROUPS, GROUP_CH, lc)
    m = jnp.stack([jnp.pad(resp[:, :, :lc - s * GROUP_CH], ((0, 0), (0, 0), (s * GROUP_CH, 0)))
                   for s in range(L)], axis=1).reshape(N_GROUPS, lc, lc)
    idx = jnp.arange(L)
    flip = (idx[:, None] + idx[None, :] == L - 1).astype(F32)
    w_re = jnp.einsum('sz,zgpd->gsdp', flip, abr, precision=hp).reshape(N_GROUPS, lc, SSM_STATE)
    w_im = jnp.einsum('sz,zgpd->gsdp', flip, abi, precision=hp).reshape(N_GROUPS, lc, SSM_STATE)
    cr = jnp.transpose(c_re, (0, 2, 1))[:, :, None, :]
    ci = jnp.transpose(c_im, (0, 2, 1))[:, :, None, :]
    qr = jnp.transpose(p_re[1:], (1, 2, 0))[..., None]
    qi = jnp.transpose(p_im[1:], (1, 2, 0))[..., None]
    v_re = (cr * qr - ci * qi).reshape(N_GROUPS, SSM_STATE, lc)
    v_im = (-cr * qi - ci * qr).reshape(N_GROUPS, SSM_STATE, lc)

    def for_length(n):
        nc, skip = n * GROUP_CH, (L - n) * GROUP_CH
        a_n = jnp.stack([p_re[n], p_im[n]], axis=1)
        return (m[:, :nc, :nc], w_re[:, skip:], w_im[:, skip:], v_re[:, :, :nc], v_im[:, :, :nc], a_n)

    return {n: for_length(n) for n in lengths}


def _ssm(u, h0, mats, layer, d_skip, batch, seq, L):
    m, w_re, w_im, v_re, v_im, a_l = mats
    depth = m.shape[0]
    gpb = GROUPS_PER_BLOCK
    n_cb = N_GROUPS // gpb
    lc = L * GROUP_CH
    state_w = gpb * SSM_STATE
    if seq == L:
        n_bblk, bp, n_steps = 1, batch, 1
    else:
        n_bblk, bp, n_steps = batch, 1, seq // L
    tok = bp * n_steps * L
    n_rows = bp * n_steps

    def to_blocks(x):
        x = x.reshape(n_bblk, bp, n_cb, state_w)
        return jnp.transpose(x, (0, 2, 1, 3))

    if h0 is None:
        h0re = h0im = jnp.zeros((n_bblk, n_cb, bp, state_w), F32)
    else:
        h0re, h0im = to_blocks(h0[..., 0]), to_blocks(h0[..., 1])
    a_blk = jnp.transpose(a_l.reshape(depth, n_cb, gpb, 2, SSM_STATE), (0, 1, 3, 2, 4))
    a_blk = a_blk.reshape(depth, n_cb, 2, state_w)

    grp = lambda a, b: pl.BlockSpec((None, gpb, a, b), lambda i, cb: (layer, cb, 0, 0))
    act = pl.BlockSpec((tok, LANES), lambda i, cb: (i, cb))
    state = pl.BlockSpec((None, None, bp, state_w), lambda i, cb: (i, cb, 0, 0))
    state_shape = jax.ShapeDtypeStruct((n_bblk, n_cb, bp, state_w), F32)
    kern = functools.partial(_ssm_kernel, L=L, n_steps=n_steps, bp=bp)
    y, hre, him = pl.pallas_call(
        kern,
        grid=(n_bblk, n_cb),
        in_specs=[act, pl.BlockSpec((1, LANES), lambda i, cb: (0, cb)),
                  grp(lc, lc), grp(lc, SSM_STATE), grp(lc, SSM_STATE),
                  grp(SSM_STATE, lc), grp(SSM_STATE, lc),
                  pl.BlockSpec((None, None, 2, state_w), lambda i, cb: (layer, cb, 0, 0)), state, state],
        out_specs=[act, state, state],
        out_shape=[jax.ShapeDtypeStruct(u.shape, F32), state_shape, state_shape],
        scratch_shapes=[pltpu.VMEM((gpb, n_rows, lc), F32)] * 2 + [pltpu.VMEM((n_rows, state_w), F32)] * 4,
        compiler_params=_params(("parallel", "parallel")),
        name="ssm",
    )(u, d_skip.reshape(1, D_SSM), m, w_re, w_im, v_re, v_im, a_blk, h0re, h0im)

    def from_blocks(x):
        return jnp.transpose(x, (0, 2, 1, 3)).reshape(batch, N_GROUPS, SSM_STATE)

    return y, jnp.stack([from_blocks(hre), from_blocks(him)], axis=-1)


def _post_kernel(x_ref, o_ref, y_ref, ga_ref, gs_ref, p_ref,
                 wao_ref, wgv_ref, wgg_ref, wout_ref, gmlp_ref, wup_ref, wdown_ref,
                 gple_ref, wple_ref, wpg_ref, out_ref, *, ff_chunk):
    def rms(x, g_ref):
        ms = jnp.mean(x * x, axis=-1, keepdims=True)
        return (x * lax.rsqrt(ms + RMS_EPS) * g_ref[...]).astype(BF16)

    y_act = jax.nn.gelu(y_ref[...], approximate=True).astype(BF16)
    attn_out = _dot(o_ref[...], wao_ref[...])
    ssm_out = _dot(y_act, wgv_ref[...]) * jax.nn.sigmoid(_dot(y_act, wgg_ref[...]))
    merged = ga_ref[...].astype(F32) * attn_out + gs_ref[...].astype(F32) * ssm_out
    x = x_ref[...] + _dot(merged.astype(BF16), wout_ref[...])
    h2 = rms(x, gmlp_ref)
    d_ff = wup_ref.shape[1]
    mlp = None
    for c0 in range(0, d_ff, ff_chunk):
        hid = jnp.square(jnp.maximum(_dot(h2, wup_ref[:, c0:c0 + ff_chunk]), 0.0)).astype(BF16)
        part = _dot(hid, wdown_ref[c0:c0 + ff_chunk, :])
        mlp = part if mlp is None else mlp + part
    x = x + mlp
    h3 = rms(x, gple_ref)
    out_ref[...] = x + _dot(p_ref[...].astype(BF16), wple_ref[...]) * jax.nn.sigmoid(_dot(h3, wpg_ref[...]))


def _post(x, o, y, ga, gs, p, w):
    m, d = x.shape
    tm = min(ROW_TILE, m)
    row = lambda a: pl.BlockSpec((tm, a.shape[1]), lambda i: (i, 0))
    vec = lambda a: a.reshape(1, -1)
    acts = (x, o, y, ga, gs, p)
    consts = (w["w_attn_out"], w["w_glu_val"], w["w_glu_gate"], w["w_out"], vec(w["g_mlp"]),
              w["w_up"], w["w_down"], vec(w["g_ple"]), w["w_ple"], w["w_ple_gate"])
    return pl.pallas_call(
        functools.partial(_post_kernel, ff_chunk=1024),
        grid=(m // tm,),
        in_specs=[row(a) for a in acts] + [_resident(c.shape) for c in consts],
        out_specs=pl.BlockSpec((tm, d), lambda i: (i, 0)),
        out_shape=jax.ShapeDtypeStruct((m, d), F32),
        compiler_params=_params(("parallel",)),
        name="post",
    )(*acts, *consts)


def kernel(x_prompt, x_sample, p_prompt, p_sample, cache_k, cache_v, state_ssm, page_table, g_mix, w_in, g_q, g_k, sb_bias, w_attn_out, ssm_a_re, ssm_a_im, ssm_log_dt, ssm_b_re, ssm_b_im, ssm_c_re, ssm_c_im, ssm_d, w_glu_val, w_glu_gate, w_out, g_mlp, w_up, w_down, g_ple, w_ple, w_ple_gate):
    depth = w_in.shape[0]
    batch, seq, d_model = x_prompt.shape
    dec_batch, dec_seq, _ = x_sample.shape
    m_p, m_s = batch * seq, dec_batch * dec_seq
    to_pages = lambda c: jnp.transpose(c, (0, 1, 3, 4, 2)).reshape(c.shape[:2] + (ATT_WIDTH, c.shape[2]))
    ck, cv = to_pages(cache_k), to_pages(cache_v)
    hd = jnp.arange(ATT_WIDTH, dtype=jnp.int32) // HEAD_DIM
    bd = jnp.where(hd[:, None] == hd[None, :], 1.0 / HEAD_DIM, 0.0).astype(BF16)

    mats = jax.vmap(lambda *p: _ssm_mats(*p, sorted({PROMPT_CHUNK, dec_seq})))(
        ssm_a_re, ssm_a_im, ssm_log_dt, ssm_b_re, ssm_b_im, ssm_c_re, ssm_c_im)

    xp, xs = x_prompt.reshape(m_p, d_model), x_sample.reshape(m_s, d_model)
    kt_all = vt_all = None
    outs = [[] for _ in range(4)]
    for i in range(depth):
        w_in_bf = w_in[i].astype(BF16)
        w = {
            "g_mix": g_mix[i], "w_in": w_in_bf, "bd": bd,
            "w_kvt": w_in_bf[:, ATT_WIDTH:3 * ATT_WIDTH].T,
            "g_q": jnp.tile(g_q[i], N_HEADS).reshape(1, ATT_WIDTH),
            "g_k": jnp.tile(g_k[i], N_HEADS).reshape(1, ATT_WIDTH),
            "w_attn_out": w_attn_out[i].astype(BF16), "w_glu_val": w_glu_val[i].astype(BF16),
            "w_glu_gate": w_glu_gate[i].astype(BF16), "w_out": w_out[i].astype(BF16),
            "g_mlp": g_mlp[i], "w_up": w_up[i].astype(BF16), "w_down": w_down[i].astype(BF16),
            "g_ple": g_ple[i], "w_ple": w_ple[i].astype(BF16), "w_ple_gate": w_ple_gate[i].astype(BF16),
        }
        bias = sb_bias[i].astype(F32)
        bias_rows = jnp.repeat(bias, dec_seq).reshape(N_HEADS * dec_seq, 1)

        q, kt_all, vt_all, u, ga, gs = _in_proj(xp, w, (i, depth, batch, seq, kt_all, vt_all))
        o = _attn_prompt(q, kt_all, vt_all, i, bias, batch, seq)
        y, h_p = _ssm(u, None, mats[PROMPT_CHUNK], i, ssm_d[i], batch, seq, PROMPT_CHUNK)
        xp = _post(xp, o, y, ga, gs, p_prompt[i].reshape(m_p, -1), w)

        q, k_s, v_s, u, ga, gs = _in_proj(xs, w)
        o = _attn_sample(q, k_s, v_s, ck, cv, i, page_table, bias_rows, dec_seq)
        y, h_s = _ssm(u, state_ssm[i], mats[dec_seq], i, ssm_d[i], dec_batch, dec_seq, dec_seq)
        xs = _post(xs, o, y, ga, gs, p_sample[i].reshape(m_s, -1), w)

        shape4 = (dec_batch, dec_seq, N_HEADS, HEAD_DIM)
        for lst, val in zip(outs, (h_p, k_s.reshape(shape4), v_s.reshape(shape4), h_s)):
            lst.append(val)
    s_p, k_s, v_s, s_s = (jnp.stack(o) for o in outs)
    from_t = lambda t: jnp.transpose(t.reshape(depth, batch, N_HEADS, HEAD_DIM, seq), (0, 1, 4, 2, 3))
    return (xp.reshape(x_prompt.shape), xs.reshape(x_sample.shape), from_t(kt_all), from_t(vt_all), s_p,
            k_s, v_s, s_s)
```

```python
import functools

import jax
import jax.numpy as jnp
from jax import lax
from jax.experimental import pallas as pl
from jax.experimental.pallas import tpu as pltpu

F32 = jnp.float32
BF16 = jnp.bfloat16

N_HEADS = 8
HEAD_DIM = 64
ATT_WIDTH = N_HEADS * HEAD_DIM
GROUP_CH = 16
N_GROUPS = 32
SSM_STATE = 64
D_SSM = N_GROUPS * GROUP_CH
RMS_EPS = 1e-6
LOG2E = 1.4426950408889634
Q_SCALE = -(HEAD_DIM ** -0.5) * LOG2E
LANES = 128
MXU_TILE = 256
GROUPS_PER_BLOCK = LANES // GROUP_CH
VMEM_LIMIT = 56 * 1024 * 1024

PROMPT_CHUNK = 16
SSM_ROW_CHUNK = 64
ATT_TK = 256
ATT_TQ = 2 * ATT_TK
ROW_TILE = 256
FF_CHUNK = 1024


def _dot(a, b):
    return jnp.dot(a, b, preferred_element_type=F32)


def _dot_nt(a, b):
    return lax.dot_general(a, b, (((1,), (1,)), ((), ())), preferred_element_type=F32)


def _split(x):
    hi = x.astype(BF16)
    lo = (x - hi.astype(F32)).astype(BF16)
    return hi, lo


def _dot3(a, b):
    ah, al = _split(a)
    bh, bl = _split(b)
    return _dot(ah, bh) + _dot(al, bh) + _dot(ah, bl)


def _params(sem):
    return pltpu.CompilerParams(dimension_semantics=sem, vmem_limit_bytes=VMEM_LIMIT)


def _resident(shape):
    nd = len(shape)
    return pl.BlockSpec(shape, lambda *_: (0,) * nd, pipeline_mode=pl.Buffered(1))


def _in_proj_kernel(x_ref, gmix_ref, w_ref, bd_ref, gq_ref, gk_ref, *rest, kv_transposed, n_prev):
    x = x_ref[...]
    ms = jnp.mean(x * x, axis=-1, keepdims=True)
    h = (x * lax.rsqrt(ms + RMS_EPS) * gmix_ref[...]).astype(BF16)
    a = ATT_WIDTH

    def head_norm(p, g_ref):
        msh = _dot((p * p).astype(BF16), bd_ref[...])
        return p * lax.rsqrt(msh + RMS_EPS) * g_ref[...]

    if kv_transposed:
        wkvt_ref = rest[0]
        q_ref, k_ref, v_ref, u_ref, ga_ref, gs_ref = rest[1 + n_prev:]
        kvt = jnp.concatenate([_dot_nt(wkvt_ref[r:r + MXU_TILE, :], h) for r in range(0, 2 * a, MXU_TILE)], axis=0)
        kt = kvt[:a]
        mst = _dot(bd_ref[...], (kt * kt).astype(BF16))
        k_ref[...] = kt * lax.rsqrt(mst + RMS_EPS) * gk_ref[...]
        v_ref[...] = kvt[a:]
    else:
        q_ref, k_ref, v_ref, u_ref, ga_ref, gs_ref = rest
        k_ref[...] = head_norm(_dot(h, w_ref[:, a:2 * a]), gk_ref)
        v_ref[...] = _dot(h, w_ref[:, 2 * a:3 * a])
    q_ref[...] = (head_norm(_dot(h, w_ref[:, 0:a]), gq_ref) * Q_SCALE).astype(BF16)
    u_ref[...] = _dot(h, w_ref[:, 3 * a:3 * a + D_SSM])
    o = 3 * a + D_SSM
    d = ga_ref.shape[-1]
    ga_ref[...] = jax.nn.sigmoid(_dot(h, w_ref[:, o:o + d])).astype(BF16)
    gs_ref[...] = jax.nn.sigmoid(_dot(h, w_ref[:, o + d:o + 2 * d])).astype(BF16)


def _in_proj(x, w, kv_slot=None):
    m, d = x.shape
    tm = min(ROW_TILE, m)
    row = lambda width: pl.BlockSpec((tm, width), lambda i: (i, 0))
    tok = lambda width, t: (row(width), jax.ShapeDtypeStruct((m, width), t))
    ins = [x, w["g_mix"].reshape(1, d), w["w_in"], w["bd"], w["g_q"]]
    in_specs = [row(d), _resident((1, d)), _resident(w["w_in"].shape), _resident(w["bd"].shape),
                _resident((1, ATT_WIDTH))]
    aliases = {}
    n_prev = 0
    if kv_slot is None:
        ins.append(w["g_k"])
        in_specs.append(_resident((1, ATT_WIDTH)))
        kv = [tok(ATT_WIDTH, F32)] * 2
    else:
        layer, depth, batch, seq, prev_kt, prev_vt = kv_slot
        per_seq = seq // tm
        ins += [w["g_k"].reshape(ATT_WIDTH, 1), w["w_kvt"]]
        in_specs += [_resident((ATT_WIDTH, 1)), _resident(w["w_kvt"].shape)]
        if prev_kt is not None:
            n_prev = 2
            aliases = {len(ins): 1, len(ins) + 1: 2}
            ins += [prev_kt, prev_vt]
            in_specs += [pl.BlockSpec(memory_space=pl.ANY)] * 2
        slot = pl.BlockSpec((None, None, ATT_WIDTH, tm), lambda i: (layer, i // per_seq, 0, i % per_seq))
        kv = [(slot, jax.ShapeDtypeStruct((depth, batch, ATT_WIDTH, seq), F32))] * 2
    outs = [tok(ATT_WIDTH, BF16)] + kv + [tok(D_SSM, F32), tok(d, BF16), tok(d, BF16)]
    return pl.pallas_call(
        functools.partial(_in_proj_kernel, kv_transposed=kv_slot is not None, n_prev=n_prev),
        grid=(m // tm,),
        in_specs=in_specs,
        out_specs=[o[0] for o in outs],
        out_shape=[o[1] for o in outs],
        input_output_aliases=aliases,
        compiler_params=_params(("parallel",)),
        name="in_proj",
    )(*ins)


def _log_stay(nz, mask):
    m = jnp.minimum(nz, 0.0)
    ls = m - jnp.log(1.0 + jnp.exp2(m + (m - nz))) * LOG2E
    return ls if mask is None else jnp.where(mask, ls, 0.0)


def _prefix_operand(ls, copies):
    return ls.astype(BF16) if copies == 1 else jnp.concatenate(_split(ls), axis=1)


def _sb_weights(ls, nz, later, carry, mask):
    w = jnp.exp2((ls - nz) + (later + carry))
    if mask is not None:
        w = jnp.where(mask, w, 0.0)
    return w, carry + (later[:, 0:1] + ls[:, 0:1])


def _sb_block(nz, mask, tri, carry):
    ls = _log_stay(nz, mask)
    later = _dot(_prefix_operand(ls, tri.shape[0] // tri.shape[1]), tri)
    return _sb_weights(ls, nz, later, carry, mask)


def _later_keys(n, copies):
    assert copies in (1, 2)
    r = lax.broadcasted_iota(jnp.int32, (copies * n, n), 0)
    c = lax.broadcasted_iota(jnp.int32, (copies * n, n), 1)
    return jnp.where(jnp.where(r >= n, r - n, r) > c, 1.0, 0.0).astype(BF16)


def _attn_prompt_kernel(bias_ref, q_ref, kt_ref, vt_ref, o_ref, kb_s, vb_s, acc_ref, carry_ref,
                        *, tq, tk, seq):
    hp = pl.program_id(1)
    qi = pl.program_id(2)
    lane = lax.broadcasted_iota(jnp.int32, (1, LANES), 1)
    first_head = lane < HEAD_DIM

    @pl.when(qi == 0)
    def _stage_keys_values():
        ones = jnp.where(lax.broadcasted_iota(jnp.int32, (LANES, tk), 0) < 2, 1.0, 0.0)
        for i in range(seq // tk):
            kb_s[i] = jnp.concatenate([kt_ref[:, i * tk:(i + 1) * tk], ones], axis=0).astype(BF16)
            vb_s[i] = vt_ref[:, i * tk:(i + 1) * tk].astype(BF16)

    qs = q_ref[...]
    q2 = jnp.concatenate([jnp.where(first_head, qs, 0), jnp.where(first_head, 0, qs)], axis=0)

    def bias_lanes(h):
        b = jnp.full((tq, LANES), bias_ref[2 * hp + h] * (-LOG2E), F32)
        b_hi = b.astype(BF16).astype(F32)
        return jnp.where(lane == 0, b_hi, jnp.where(lane == 1, b - b_hi, 0.0)).astype(BF16)
    q2 = jnp.concatenate([q2, jnp.concatenate([bias_lanes(0), bias_lanes(1)], axis=0)], axis=1)
    tri = _later_keys(tk, 1)

    def block(kb, mask, carry, q_rows):
        n = q_rows.shape[0] // 2
        nz = _dot(q_rows, kb_s[kb])
        w, carry = _sb_block(nz, mask, tri, carry)
        pv = _dot_nt(w.astype(BF16), vb_s[kb])
        return jnp.where(first_head, pv[:n], pv[n:]), carry

    assert tq == 2 * tk
    top = 2 * qi + 1
    r = lax.broadcasted_iota(jnp.int32, (2 * tk, tk), 0)
    c = lax.broadcasted_iota(jnp.int32, (2 * tk, tk), 1)
    triangle = c < jnp.where(r < tk, r, r - tk)
    half = lambda x, lo: jnp.concatenate([x[lo:lo + tk], x[tq + lo:tq + lo + tk]], axis=0)
    q_early, q_late = half(q2, 0), half(q2, tk)
    zero = jnp.zeros((2 * tk, 1), F32)
    late_top, c_late = block(top, triangle, zero, q_late)
    late_full, c_late = block(top - 1, None, c_late, q_late)
    early, c_early = block(top - 1, triangle, zero, q_early)
    acc_ref[0:tk, :] = early
    acc_ref[tk:tq, :] = late_top + late_full
    carry_ref[...] = jnp.concatenate([c_early[:tk], c_late[:tk], c_early[tk:], c_late[tk:]], axis=0)

    def below(first, n):
        carry = carry_ref[...]
        total = None
        for d in range(n):
            contrib, carry = block(first - d, None, carry, q2)
            total = contrib if total is None else total + contrib
        acc_ref[...] += total
        carry_ref[...] = carry

    odd = jnp.bitwise_and(qi, 1)

    @pl.when(odd == 1)
    def _pair():
        below(top - 2, 2)

    def four(i, _):
        below(top - 2 - 2 * odd - 4 * i, 4)
        return 0
    lax.fori_loop(0, lax.shift_right_logical(qi, 1), four, 0)
    o_ref[...] = acc_ref[...].astype(BF16)


def _attn_prompt(q, kt_all, vt_all, layer, bias, batch, seq):
    tq, tk = ATT_TQ, ATT_TK
    nq = seq // tq
    kern = functools.partial(_attn_prompt_kernel, tq=tq, tk=tk, seq=seq)
    head_pair = pl.BlockSpec((None, None, LANES, seq), lambda b, hp, i: (layer, b, hp, 0))
    return pl.pallas_call(
        kern,
        grid=(batch, N_HEADS // 2, nq),
        in_specs=[pl.BlockSpec(memory_space=pltpu.SMEM),
                  pl.BlockSpec((tq, LANES), lambda b, hp, i: (b * nq + i, hp)),
                  head_pair, head_pair],
        out_specs=pl.BlockSpec((tq, LANES), lambda b, hp, i: (b * nq + i, hp)),
        out_shape=jax.ShapeDtypeStruct(q.shape, BF16),
        scratch_shapes=[pltpu.VMEM((seq // tk, 2 * LANES, tk), BF16), pltpu.VMEM((seq // tk, LANES, tk), BF16)]
                       + [pltpu.VMEM((tq, LANES), F32), pltpu.VMEM((2 * tq, 1), F32)],
        compiler_params=_params(("parallel", "parallel", "arbitrary")),
        name="attn_prompt",
    )(bias, q, kt_all, vt_all)


def _attn_sample_kernel(pt_ref, bias_ref, hmask_ref, nmask_ref, q_ref, kn_ref, vn_ref, *rest,
                        n_pages, t_new, page):
    del pt_ref
    k_pages = rest[:n_pages]
    v_pages = rest[n_pages:2 * n_pages]
    o_ref = rest[2 * n_pages]
    rows = N_HEADS * t_new
    same_head = hmask_ref[...] > 0.5
    q = q_ref[...].astype(F32)
    q_bd = jnp.where(same_head, jnp.concatenate([q] * N_HEADS, axis=0), 0.0).astype(BF16)
    nbias = bias_ref[...] * (-LOG2E)

    pad = jnp.zeros((page - t_new, ATT_WIDTH), F32)
    k_new = jnp.concatenate([kn_ref[...], pad], axis=0).astype(BF16)
    v_new = jnp.concatenate([vn_ref[...], pad], axis=0).astype(BF16)
    new_mask = nmask_ref[...] > 0.5
    nz = _dot_nt(q_bd, k_new) + nbias
    w, carry = _sb_block(nz, new_mask, _later_keys(page, 2), jnp.zeros((rows, 1), F32))
    acc = _dot(w.astype(BF16), v_new)

    per_block = 2 if n_pages % 2 == 0 else 1
    bw = per_block * page
    n_blk = n_pages // per_block
    all_pages = lambda refs: jnp.concatenate([r[...] for r in refs], axis=1).astype(BF16)
    nz = _dot(q_bd, all_pages(k_pages)) + nbias
    ls = _log_stay(nz, None)
    blocks = [slice(jb * bw, (jb + 1) * bw) for jb in range(n_blk)]
    later = _dot(jnp.concatenate([_prefix_operand(ls[:, b], 2) for b in blocks], axis=0), _later_keys(bw, 2))
    ws = [None] * n_blk
    for jb in reversed(range(n_blk)):
        b = blocks[jb]
        w, carry = _sb_weights(ls[:, b], nz[:, b], later[jb * rows:(jb + 1) * rows], carry, None)
        ws[jb] = w.astype(BF16)
    acc = acc + _dot_nt(jnp.concatenate(ws, axis=1), all_pages(v_pages))
    acc = jnp.where(same_head, acc, 0.0)
    out = acc[0:t_new, :]
    for h in range(1, N_HEADS):
        out = out + acc[h * t_new:(h + 1) * t_new, :]
    o_ref[...] = out.astype(BF16)


def _attn_sample(q, k_new, v_new, cache_k, cache_v, layer, page_table, bias_rows, t_new):
    n_seq, n_pages = page_table.shape
    page = cache_k.shape[3]
    page_spec = lambda j: pl.BlockSpec((None, None, ATT_WIDTH, page),
                                       lambda b, pt, j=j: (layer, pt[b, j], 0, 0))
    tok = pl.BlockSpec((t_new, ATT_WIDTH), lambda b, pt: (b, 0))
    seq_blk = pl.BlockSpec((None, t_new, ATT_WIDTH), lambda b, pt: (b, 0, 0))
    rows = N_HEADS * t_new
    whole = lambda w: pl.BlockSpec((rows, w), lambda b, pt: (0, 0))
    row_head = jnp.arange(rows, dtype=jnp.int32)[:, None] // t_new
    row_query = jnp.arange(rows, dtype=jnp.int32)[:, None] % t_new
    head_mask = (row_head == jnp.arange(ATT_WIDTH, dtype=jnp.int32)[None, :] // HEAD_DIM).astype(F32)
    new_mask = (jnp.arange(page, dtype=jnp.int32)[None, :] < row_query).astype(F32)
    kern = functools.partial(_attn_sample_kernel, n_pages=n_pages, t_new=t_new, page=page)
    grid_spec = pltpu.PrefetchScalarGridSpec(
        num_scalar_prefetch=1,
        grid=(n_seq,),
        in_specs=[whole(1), whole(ATT_WIDTH), whole(page), seq_blk, tok, tok]
                 + [page_spec(j) for j in range(n_pages)] * 2,
        out_specs=seq_blk,
    )
    o = pl.pallas_call(
        kern,
        grid_spec=grid_spec,
        out_shape=jax.ShapeDtypeStruct((n_seq, t_new, ATT_WIDTH), BF16),
        compiler_params=_params(("arbitrary",)),
        name="attn_sample",
    )(page_table, bias_rows, head_mask, new_mask, q.reshape(n_seq, t_new, ATT_WIDTH), k_new, v_new,
      *([cache_k] * n_pages), *([cache_v] * n_pages))
    return o.reshape(q.shape)


def _ssm_kernel(u_ref, d_ref, m_ref, wre_ref, wim_ref, vre_ref, vim_ref, a_ref, h0re_ref, h0im_ref,
                y_ref, hre_ref, him_ref, ug_s, yg_s, zre_s, zim_s, inre_s, inim_s, *, L, n_steps, bp):
    n_rows = n_steps * bp
    p = SSM_STATE
    gpb = GROUPS_PER_BLOCK
    slot = lax.shift_right_logical(lax.broadcasted_iota(jnp.int32, (1, LANES), 1), GROUP_CH.bit_length() - 1)

    def transpose_slots(a):
        a = list(a)
        k = gpb // 2
        while k:
            upper = jnp.bitwise_and(slot, k) != 0
            for i in range(gpb):
                if not i & k:
                    lo, hi = a[i], a[i + k]
                    a[i] = jnp.where(upper, pltpu.roll(hi, k * GROUP_CH, 1), lo)
                    a[i + k] = jnp.where(upper, hi, pltpu.roll(lo, LANES - k * GROUP_CH, 1))
            k //= 2
        return a

    rc = min(n_rows, SSM_ROW_CHUNK)
    assert n_rows % rc == 0 and L % gpb == 0
    n_half = L // gpb
    for r0 in range(0, n_rows, rc):
        for h in range(n_half):
            steps = [u_ref[pl.ds(r0 * L + h * gpb + j, rc, stride=L), :] for j in range(gpb)]
            for g, u_gh in enumerate(transpose_slots(steps)):
                ug_s[g, r0:r0 + rc, h * LANES:(h + 1) * LANES] = u_gh
    for g in range(gpb):
        u_g = ug_s[g]
        zre_s[:, g * p:(g + 1) * p] = _dot3(u_g, wre_ref[g])
        zim_s[:, g * p:(g + 1) * p] = _dot3(u_g, wim_ref[g])

    ar = a_ref[0:1, :]
    ai = a_ref[1:2, :]

    def step(k, carry):
        hr, hi = carry
        rows = pl.ds(k * bp, bp)
        inre_s[rows, :] = hr
        inim_s[rows, :] = hi
        zr = zre_s[rows, :]
        zi = zim_s[rows, :]
        return ar * hr - ai * hi + zr, ar * hi + ai * hr + zi

    hr, hi = lax.fori_loop(0, n_steps, step, (h0re_ref[...], h0im_ref[...]))
    hre_ref[...] = hr
    him_ref[...] = hi

    bdot = lambda a, b: _dot(a.astype(BF16), b.astype(BF16))
    for g in range(gpb):
        yg_s[g] = (bdot(ug_s[g], m_ref[g]) + bdot(inre_s[:, g * p:(g + 1) * p], vre_ref[g])
                   + bdot(inim_s[:, g * p:(g + 1) * p], vim_ref[g]))
    for r0 in range(0, n_rows, rc):
        for h in range(n_half):
            groups = [yg_s[g, r0:r0 + rc, h * LANES:(h + 1) * LANES] for g in range(gpb)]
            for j, y_step in enumerate(transpose_slots(groups)):
                rows = pl.ds(r0 * L + h * gpb + j, rc, stride=L)
                y_ref[rows, :] = y_step + d_ref[...] * u_ref[rows, :]


def _ssm_mats(a_re, a_im, log_dt, b_re, b_im, c_re, c_im, lengths):
    hp = lax.Precision.HIGHEST
    L = max(lengths)
    dt = jnp.exp(log_dt)[:, None]
    mag = jnp.exp(a_re * dt)
    ab_re, ab_im = mag * jnp.cos(a_im * dt), mag * jnp.sin(a_im * dt)
    den = a_re * a_re + a_im * a_im
    nr, ni = ab_re - 1.0, ab_im
    f_re = (nr * a_re + ni * a_im) / den
    f_im = (ni * a_re - nr * a_im) / den
    bb_re = f_re[..., None] * b_re - f_im[..., None] * b_im
    bb_im = f_re[..., None] * b_im + f_im[..., None] * b_re
    pr, pi = [jnp.ones_like(ab_re)], [jnp.zeros_like(ab_re)]
    for _ in range(L):
        pr, pi = pr + [ab_re * pr[-1] - ab_im * pi[-1]], pi + [ab_re * pi[-1] + ab_im * pr[-1]]
    p_re, p_im = jnp.stack(pr), jnp.stack(pi)
    abr = p_re[:L, :, :, None] * bb_re - p_im[:L, :, :, None] * bb_im
    abi = p_re[:L, :, :, None] * bb_im + p_im[:L, :, :, None] * bb_re
    kern = (jnp.einsum('gcp,tgpd->tgcd', c_re, abr, precision=hp)
            - jnp.einsum('gcp,tgpd->tgcd', c_im, abi, precision=hp))
    lc = L * GROUP_CH
    resp = jnp.transpose(kern, (1, 3, 0, 2)).reshape(N_GROUPS, GROUP_CH, lc)
    m = jnp.stack([jnp.pad(resp[:, :, :lc - s * GROUP_CH], ((0, 0), (0, 0), (s * GROUP_CH, 0)))
                   for s in range(L)], axis=1).reshape(N_GROUPS, lc, lc)
    idx = jnp.arange(L)
    flip = (idx[:, None] + idx[None, :] == L - 1).astype(F32)
    w_re = jnp.einsum('sz,zgpd->gsdp', flip, abr, precision=hp).reshape(N_GROUPS, lc, SSM_STATE)
    w_im = jnp.einsum('sz,zgpd->gsdp', flip, abi, precision=hp).reshape(N_GROUPS, lc, SSM_STATE)
    cr = jnp.transpose(c_re, (0, 2, 1))[:, :, None, :]
    ci = jnp.transpose(c_im, (0, 2, 1))[:, :, None, :]
    qr = jnp.transpose(p_re[1:], (1, 2, 0))[..., None]
    qi = jnp.transpose(p_im[1:], (1, 2, 0))[..., None]
    v_re = (cr * qr - ci * qi).reshape(N_GROUPS, SSM_STATE, lc)
    v_im = (-cr * qi - ci * qr).reshape(N_GROUPS, SSM_STATE, lc)

    def for_length(n):
        nc, skip = n * GROUP_CH, (L - n) * GROUP_CH
        a_n = jnp.stack([p_re[n], p_im[n]], axis=1)
        return (m[:, :nc, :nc], w_re[:, skip:], w_im[:, skip:], v_re[:, :, :nc], v_im[:, :, :nc], a_n)

    return {n: for_length(n) for n in lengths}


def _ssm(u, h0, mats, layer, d_skip, batch, seq, L):
    m, w_re, w_im, v_re, v_im, a_l = mats
    depth = m.shape[0]
    gpb = GROUPS_PER_BLOCK
    n_cb = N_GROUPS // gpb
    lc = L * GROUP_CH
    state_w = gpb * SSM_STATE
    if seq == L:
        n_bblk, bp, n_steps = 1, batch, 1
    else:
        n_bblk, bp, n_steps = batch, 1, seq // L
    tok = bp * n_steps * L
    n_rows = bp * n_steps

    def to_blocks(x):
        x = x.reshape(n_bblk, bp, n_cb, state_w)
        return jnp.transpose(x, (0, 2, 1, 3))

    if h0 is None:
        h0re = h0im = jnp.zeros((n_bblk, n_cb, bp, state_w), F32)
    else:
        h0re, h0im = to_blocks(h0[..., 0]), to_blocks(h0[..., 1])
    a_blk = jnp.transpose(a_l.reshape(depth, n_cb, gpb, 2, SSM_STATE), (0, 1, 3, 2, 4))
    a_blk = a_blk.reshape(depth, n_cb, 2, state_w)

    grp = lambda a, b: pl.BlockSpec((None, gpb, a, b), lambda i, cb: (layer, cb, 0, 0))
    act = pl.BlockSpec((tok, LANES), lambda i, cb: (i, cb))
    state = pl.BlockSpec((None, None, bp, state_w), lambda i, cb: (i, cb, 0, 0))
    state_shape = jax.ShapeDtypeStruct((n_bblk, n_cb, bp, state_w), F32)
    kern = functools.partial(_ssm_kernel, L=L, n_steps=n_steps, bp=bp)
    y, hre, him = pl.pallas_call(
        kern,
        grid=(n_bblk, n_cb),
        in_specs=[act, pl.BlockSpec((1, LANES), lambda i, cb: (0, cb)),
                  grp(lc, lc), grp(lc, SSM_STATE), grp(lc, SSM_STATE),
                  grp(SSM_STATE, lc), grp(SSM_STATE, lc),
                  pl.BlockSpec((None, None, 2, state_w), lambda i, cb: (layer, cb, 0, 0)), state, state],
        out_specs=[act, state, state],
        out_shape=[jax.ShapeDtypeStruct(u.shape, F32), state_shape, state_shape],
        scratch_shapes=[pltpu.VMEM((gpb, n_rows, lc), F32)] * 2 + [pltpu.VMEM((n_rows, state_w), F32)] * 4,
        compiler_params=_params(("parallel", "parallel")),
        name="ssm",
    )(u, d_skip.reshape(1, D_SSM), m, w_re, w_im, v_re, v_im, a_blk, h0re, h0im)

    def from_blocks(x):
        return jnp.transpose(x, (0, 2, 1, 3)).reshape(batch, N_GROUPS, SSM_STATE)

    return y, jnp.stack([from_blocks(hre), from_blocks(him)], axis=-1)


def _post_kernel(x_ref, o_ref, y_ref, ga_ref, gs_ref, p_ref,
                 wao_ref, wgv_ref, wgg_ref, wout_ref, gmlp_ref, wup_ref, wdown_ref,
                 gple_ref, wple_ref, wpg_ref, out_ref, *, ff_chunk):
    def rms(x, g_ref):
        ms = jnp.mean(x * x, axis=-1, keepdims=True)
        return (x * lax.rsqrt(ms + RMS_EPS) * g_ref[...]).astype(BF16)

    y_act = jax.nn.gelu(y_ref[...], approximate=True).astype(BF16)
    attn_out = _dot(o_ref[...], wao_ref[...])
    ssm_out = _dot(y_act, wgv_ref[...]) * jax.nn.sigmoid(_dot(y_act, wgg_ref[...]))
    merged = ga_ref[...].astype(F32) * attn_out + gs_ref[...].astype(F32) * ssm_out
    x = x_ref[...] + _dot(merged.astype(BF16), wout_ref[...])
    h2 = rms(x, gmlp_ref)
    d_ff = wup_ref.shape[1]
    mlp = None
    for c0 in range(0, d_ff, ff_chunk):
        hid = jnp.square(jnp.maximum(_dot(h2, wup_ref[:, c0:c0 + ff_chunk]), 0.0)).astype(BF16)
        part = _dot(hid, wdown_ref[c0:c0 + ff_chunk, :])
        mlp = part if mlp is None else mlp + part
    x = x + mlp
    h3 = rms(x, gple_ref)
    out_ref[...] = x + _dot(p_ref[...].astype(BF16), wple_ref[...]) * jax.nn.sigmoid(_dot(h3, wpg_ref[...]))


def _post(x, o, y, ga, gs, p, w):
    m, d = x.shape
    tm = min(ROW_TILE, m)
    row = lambda a: pl.BlockSpec((tm, a.shape[1]), lambda i: (i, 0))
    vec = lambda a: a.reshape(1, -1)
    acts = (x, o, y, ga, gs, p)
    consts = (w["w_attn_out"], w["w_glu_val"], w["w_glu_gate"], w["w_out"], vec(w["g_mlp"]),
              w["w_up"], w["w_down"], vec(w["g_ple"]), w["w_ple"], w["w_ple_gate"])
    return pl.pallas_call(
        functools.partial(_post_kernel, ff_chunk=FF_CHUNK),
        grid=(m // tm,),
        in_specs=[row(a) for a in acts] + [_resident(c.shape) for c in consts],
        out_specs=pl.BlockSpec((tm, d), lambda i: (i, 0)),
        out_shape=jax.ShapeDtypeStruct((m, d), F32),
        compiler_params=_params(("parallel",)),
        name="post",
    )(*acts, *consts)


def kernel(x_prompt, x_sample, p_prompt, p_sample, cache_k, cache_v, state_ssm, page_table, g_mix, w_in, g_q, g_k, sb_bias, w_attn_out, ssm_a_re, ssm_a_im, ssm_log_dt, ssm_b_re, ssm_b_im, ssm_c_re, ssm_c_im, ssm_d, w_glu_val, w_glu_gate, w_out, g_mlp, w_up, w_down, g_ple, w_ple, w_ple_gate):
    depth = w_in.shape[0]
    batch, seq, d_model = x_prompt.shape
    dec_batch, dec_seq, _ = x_sample.shape
    m_p, m_s = batch * seq, dec_batch * dec_seq
    to_pages = lambda c: jnp.transpose(c, (0, 1, 3, 4, 2)).reshape(c.shape[:2] + (ATT_WIDTH, c.shape[2]))
    ck, cv = to_pages(cache_k), to_pages(cache_v)
    hd = jnp.arange(ATT_WIDTH, dtype=jnp.int32) // HEAD_DIM
    bd = jnp.where(hd[:, None] == hd[None, :], 1.0 / HEAD_DIM, 0.0).astype(BF16)

    mats = jax.vmap(lambda *p: _ssm_mats(*p, sorted({PROMPT_CHUNK, dec_seq})))(
        ssm_a_re, ssm_a_im, ssm_log_dt, ssm_b_re, ssm_b_im, ssm_c_re, ssm_c_im)

    xp, xs = x_prompt.reshape(m_p, d_model), x_sample.reshape(m_s, d_model)
    kt_all = vt_all = None
    outs = [[] for _ in range(4)]
    for i in range(depth):
        w_in_bf = w_in[i].astype(BF16)
        w = {
            "g_mix": g_mix[i], "w_in": w_in_bf, "bd": bd,
            "w_kvt": w_in_bf[:, ATT_WIDTH:3 * ATT_WIDTH].T,
            "g_q": jnp.tile(g_q[i], N_HEADS).reshape(1, ATT_WIDTH),
            "g_k": jnp.tile(g_k[i], N_HEADS).reshape(1, ATT_WIDTH),
            "w_attn_out": w_attn_out[i].astype(BF16), "w_glu_val": w_glu_val[i].astype(BF16),
            "w_glu_gate": w_glu_gate[i].astype(BF16), "w_out": w_out[i].astype(BF16),
            "g_mlp": g_mlp[i], "w_up": w_up[i].astype(BF16), "w_down": w_down[i].astype(BF16),
            "g_ple": g_ple[i], "w_ple": w_ple[i].astype(BF16), "w_ple_gate": w_ple_gate[i].astype(BF16),
        }
        bias = sb_bias[i].astype(F32)
        bias_rows = jnp.repeat(bias, dec_seq).reshape(N_HEADS * dec_seq, 1)

        q, kt_all, vt_all, u, ga, gs = _in_proj(xp, w, (i, depth, batch, seq, kt_all, vt_all))
        o = _attn_prompt(q, kt_all, vt_all, i, bias, batch, seq)
        y, h_p = _ssm(u, None, mats[PROMPT_CHUNK], i, ssm_d[i], batch, seq, PROMPT_CHUNK)
        xp = _post(xp, o, y, ga, gs, p_prompt[i].reshape(m_p, -1), w)

        q, k_s, v_s, u, ga, gs = _in_proj(xs, w)
        o = _attn_sample(q, k_s, v_s, ck, cv, i, page_table, bias_rows, dec_seq)
        y, h_s = _ssm(u, state_ssm[i], mats[dec_seq], i, ssm_d[i], dec_batch, dec_seq, dec_seq)
        xs = _post(xs, o, y, ga, gs, p_sample[i].reshape(m_s, -1), w)

        shape4 = (dec_batch, dec_seq, N_HEADS, HEAD_DIM)
        for lst, val in zip(outs, (h_p, k_s.reshape(shape4), v_s.reshape(shape4), h_s)):
            lst.append(val)
    s_p, k_s, v_s, s_s = (jnp.stack(o) for o in outs)
    from_t = lambda t: jnp.transpose(t.reshape(depth, batch, N_HEADS, HEAD_DIM, seq), (0, 1, 4, 2, 3))
    return (xp.reshape(x_prompt.shape), xs.reshape(x_sample.shape), from_t(kt_all), from_t(vt_all), s_p,
            k_s, v_s, s_s)
```

```python
import functools

import jax
import jax.numpy as jnp
from jax import lax
from jax.experimental import pallas as pl
from jax.experimental.pallas import tpu as pltpu

F32 = jnp.float32
BF16 = jnp.bfloat16

N_HEADS = 8
HEAD_DIM = 64
ATT_WIDTH = N_HEADS * HEAD_DIM
GROUP_CH = 16
N_GROUPS = 32
SSM_STATE = 64
D_SSM = N_GROUPS * GROUP_CH
RMS_EPS = 1e-6
LOG2E = 1.4426950408889634
Q_SCALE = -(HEAD_DIM ** -0.5) * LOG2E
LANES = 128
MXU_TILE = 256
GROUPS_PER_BLOCK = LANES // GROUP_CH
VMEM_LIMIT = 56 * 1024 * 1024

PROMPT_CHUNK = 16
SSM_ROW_CHUNK = 64
ATT_TK = 256
ATT_TQ = 2 * ATT_TK
ROW_TILE = 256
FF_CHUNK = 1024


def _dot(a, b):
    return jnp.dot(a, b, preferred_element_type=F32)


def _dot_nt(a, b):
    return lax.dot_general(a, b, (((1,), (1,)), ((), ())), preferred_element_type=F32)


def _split(x):
    hi = x.astype(BF16)
    lo = (x - hi.astype(F32)).astype(BF16)
    return hi, lo


def _dot3(a, b):
    ah, al = _split(a)
    bh, bl = _split(b)
    return _dot(ah, bh) + _dot(al, bh) + _dot(ah, bl)


def _params(sem):
    return pltpu.CompilerParams(dimension_semantics=sem, vmem_limit_bytes=VMEM_LIMIT)


def _resident(shape):
    nd = len(shape)
    return pl.BlockSpec(shape, lambda *_: (0,) * nd, pipeline_mode=pl.Buffered(1))


def _in_proj_kernel(x_ref, gmix_ref, w_ref, bd_ref, gq_ref, gk_ref, *rest, kv_transposed, n_prev):
    x = x_ref[...]
    ms = jnp.mean(x * x, axis=-1, keepdims=True)
    h = (x * lax.rsqrt(ms + RMS_EPS) * gmix_ref[...]).astype(BF16)
    a = ATT_WIDTH

    def head_norm(p, g_ref):
        msh = _dot((p * p).astype(BF16), bd_ref[...])
        return p * lax.rsqrt(msh + RMS_EPS) * g_ref[...]

    if kv_transposed:
        wkvt_ref = rest[0]
        q_ref, k_ref, v_ref, u_ref, ga_ref, gs_ref = rest[1 + n_prev:]
        kvt = jnp.concatenate([_dot_nt(wkvt_ref[r:r + MXU_TILE, :], h) for r in range(0, 2 * a, MXU_TILE)], axis=0)
        kt = kvt[:a]
        mst = _dot(bd_ref[...], (kt * kt).astype(BF16))
        k_ref[...] = kt * lax.rsqrt(mst + RMS_EPS) * gk_ref[...]
        v_ref[...] = kvt[a:]
    else:
        q_ref, k_ref, v_ref, u_ref, ga_ref, gs_ref = rest
        k_ref[...] = head_norm(_dot(h, w_ref[:, a:2 * a]), gk_ref)
        v_ref[...] = _dot(h, w_ref[:, 2 * a:3 * a])
    q_ref[...] = (head_norm(_dot(h, w_ref[:, 0:a]), gq_ref) * Q_SCALE).astype(BF16)
    u_ref[...] = _dot(h, w_ref[:, 3 * a:3 * a + D_SSM])
    o = 3 * a + D_SSM
    d = ga_ref.shape[-1]
    ga_ref[...] = jax.nn.sigmoid(_dot(h, w_ref[:, o:o + d])).astype(BF16)
    gs_ref[...] = jax.nn.sigmoid(_dot(h, w_ref[:, o + d:o + 2 * d])).astype(BF16)


def _in_proj(x, w, kv_slot=None):
    m, d = x.shape
    tm = min(ROW_TILE, m)
    row = lambda width: pl.BlockSpec((tm, width), lambda i: (i, 0))
    tok = lambda width, t: (row(width), jax.ShapeDtypeStruct((m, width), t))
    ins = [x, w["g_mix"].reshape(1, d), w["w_in"], w["bd"], w["g_q"]]
    in_specs = [row(d), _resident((1, d)), _resident(w["w_in"].shape), _resident(w["bd"].shape),
                _resident((1, ATT_WIDTH))]
    aliases = {}
    n_prev = 0
    if kv_slot is None:
        ins.append(w["g_k"])
        in_specs.append(_resident((1, ATT_WIDTH)))
        kv = [tok(ATT_WIDTH, F32)] * 2
    else:
        layer, depth, batch, seq, prev_kt, prev_vt = kv_slot
        per_seq = seq // tm
        ins += [w["g_k"].reshape(ATT_WIDTH, 1), w["w_kvt"]]
        in_specs += [_resident((ATT_WIDTH, 1)), _resident(w["w_kvt"].shape)]
        if prev_kt is not None:
            n_prev = 2
            aliases = {len(ins): 1, len(ins) + 1: 2}
            ins += [prev_kt, prev_vt]
            in_specs += [pl.BlockSpec(memory_space=pl.ANY)] * 2
        slot = pl.BlockSpec((None, None, ATT_WIDTH, tm), lambda i: (layer, i // per_seq, 0, i % per_seq))
        kv = [(slot, jax.ShapeDtypeStruct((depth, batch, ATT_WIDTH, seq), F32))] * 2
    outs = [tok(ATT_WIDTH, BF16)] + kv + [tok(D_SSM, F32), tok(d, BF16), tok(d, BF16)]
    return pl.pallas_call(
        functools.partial(_in_proj_kernel, kv_transposed=kv_slot is not None, n_prev=n_prev),
        grid=(m // tm,),
        in_specs=in_specs,
        out_specs=[o[0] for o in outs],
        out_shape=[o[1] for o in outs],
        input_output_aliases=aliases,
        compiler_params=_params(("parallel",)),
        name="in_proj",
    )(*ins)


def _log_stay(nz, mask):
    m = jnp.minimum(nz, 0.0)
    ls = m - jnp.log(1.0 + jnp.exp2(m + (m - nz))) * LOG2E
    return ls if mask is None else jnp.where(mask, ls, 0.0)


def _prefix_operand(ls, copies):
    return ls.astype(BF16) if copies == 1 else jnp.concatenate(_split(ls), axis=1)


def _sb_weights(ls, nz, later, carry, mask):
    after = later + carry
    w = jnp.exp2((ls - nz) + after)
    if mask is not None:
        w = jnp.where(mask, w, 0.0)
    return w, after[:, 0:1] + ls[:, 0:1]


def _sb_block(nz, mask, tri, carry):
    ls = _log_stay(nz, mask)
    later = _dot(_prefix_operand(ls, tri.shape[0] // tri.shape[1]), tri)
    return _sb_weights(ls, nz, later, carry, mask)


def _later_keys(n, copies):
    assert copies in (1, 2)
    r = lax.broadcasted_iota(jnp.int32, (copies * n, n), 0)
    c = lax.broadcasted_iota(jnp.int32, (copies * n, n), 1)
    return jnp.where(jnp.where(r >= n, r - n, r) > c, 1.0, 0.0).astype(BF16)


def _attn_prompt_kernel(bias_ref, q_ref, kt_ref, vt_ref, o_ref, kb_s, vb_s, acc_ref, carry_ref,
                        *, tq, tk, seq):
    hp = pl.program_id(1)
    qi = pl.program_id(2)
    lane = lax.broadcasted_iota(jnp.int32, (1, LANES), 1)
    first_head = lane < HEAD_DIM

    @pl.when(qi == 0)
    def _stage_keys_values():
        ones = jnp.where(lax.broadcasted_iota(jnp.int32, (LANES, tk), 0) < 2, 1.0, 0.0)
        for i in range(seq // tk):
            kb_s[i] = jnp.concatenate([kt_ref[:, i * tk:(i + 1) * tk], ones], axis=0).astype(BF16)
            vb_s[i] = vt_ref[:, i * tk:(i + 1) * tk].astype(BF16)

    qs = q_ref[...]
    q2 = jnp.concatenate([jnp.where(first_head, qs, 0), jnp.where(first_head, 0, qs)], axis=0)

    def bias_lanes(h):
        b = jnp.full((tq, LANES), bias_ref[2 * hp + h] * (-LOG2E), F32)
        b_hi = b.astype(BF16).astype(F32)
        return jnp.where(lane == 0, b_hi, jnp.where(lane == 1, b - b_hi, 0.0)).astype(BF16)
    q2 = jnp.concatenate([q2, jnp.concatenate([bias_lanes(0), bias_lanes(1)], axis=0)], axis=1)
    tri = _later_keys(tk, 1)

    def block(kb, mask, carry, q_rows):
        n = q_rows.shape[0] // 2
        nz = _dot(q_rows, kb_s[kb])
        w, carry = _sb_block(nz, mask, tri, carry)
        pv = _dot_nt(w.astype(BF16), vb_s[kb])
        return jnp.where(first_head, pv[:n], pv[n:]), carry

    assert tq == 2 * tk
    top = 2 * qi + 1
    r = lax.broadcasted_iota(jnp.int32, (2 * tk, tk), 0)
    c = lax.broadcasted_iota(jnp.int32, (2 * tk, tk), 1)
    triangle = c < jnp.where(r < tk, r, r - tk)
    half = lambda x, lo: jnp.concatenate([x[lo:lo + tk], x[tq + lo:tq + lo + tk]], axis=0)
    q_early, q_late = half(q2, 0), half(q2, tk)
    zero = jnp.zeros((2 * tk, 1), F32)
    late_top, c_late = block(top, triangle, zero, q_late)
    late_full, c_late = block(top - 1, None, c_late, q_late)
    early, c_early = block(top - 1, triangle, zero, q_early)
    acc_ref[0:tk, :] = early
    acc_ref[tk:tq, :] = late_top + late_full
    carry_ref[...] = jnp.concatenate([c_early[:tk], c_late[:tk], c_early[tk:], c_late[tk:]], axis=0)

    def below(first, n):
        carry = carry_ref[...]
        total = None
        for d in range(n):
            contrib, carry = block(first - d, None, carry, q2)
            total = contrib if total is None else total + contrib
        acc_ref[...] += total
        carry_ref[...] = carry

    odd = jnp.bitwise_and(qi, 1)

    @pl.when(odd == 1)
    def _pair():
        below(top - 2, 2)

    def four(i, _):
        below(top - 2 - 2 * odd - 4 * i, 4)
        return 0
    lax.fori_loop(0, lax.shift_right_logical(qi, 1), four, 0)
    o_ref[...] = acc_ref[...].astype(BF16)


def _attn_prompt(q, kt_all, vt_all, layer, bias, batch, seq):
    tq, tk = ATT_TQ, ATT_TK
    nq = seq // tq
    kern = functools.partial(_attn_prompt_kernel, tq=tq, tk=tk, seq=seq)
    head_pair = pl.BlockSpec((None, None, LANES, seq), lambda b, hp, i: (layer, b, hp, 0))
    return pl.pallas_call(
        kern,
        grid=(batch, N_HEADS // 2, nq),
        in_specs=[pl.BlockSpec(memory_space=pltpu.SMEM),
                  pl.BlockSpec((tq, LANES), lambda b, hp, i: (b * nq + i, hp)),
                  head_pair, head_pair],
        out_specs=pl.BlockSpec((tq, LANES), lambda b, hp, i: (b * nq + i, hp)),
        out_shape=jax.ShapeDtypeStruct(q.shape, BF16),
        scratch_shapes=[pltpu.VMEM((seq // tk, 2 * LANES, tk), BF16), pltpu.VMEM((seq // tk, LANES, tk), BF16)]
                       + [pltpu.VMEM((tq, LANES), F32), pltpu.VMEM((2 * tq, 1), F32)],
        compiler_params=_params(("parallel", "parallel", "arbitrary")),
        name="attn_prompt",
    )(bias, q, kt_all, vt_all)


def _attn_sample_kernel(pt_ref, bias_ref, hmask_ref, nmask_ref, q_ref, kn_ref, vn_ref, *rest,
                        n_pages, t_new, page):
    del pt_ref
    k_pages = rest[:n_pages]
    v_pages = rest[n_pages:2 * n_pages]
    o_ref = rest[2 * n_pages]
    rows = N_HEADS * t_new
    same_head = hmask_ref[...] > 0.5
    q = q_ref[...].astype(F32)
    q_bd = jnp.where(same_head, jnp.concatenate([q] * N_HEADS, axis=0), 0.0).astype(BF16)
    nbias = bias_ref[...] * (-LOG2E)

    pad = jnp.zeros((page - t_new, ATT_WIDTH), F32)
    k_new = jnp.concatenate([kn_ref[...], pad], axis=0).astype(BF16)
    v_new = jnp.concatenate([vn_ref[...], pad], axis=0).astype(BF16)
    new_mask = nmask_ref[...] > 0.5
    nz = _dot_nt(q_bd, k_new) + nbias
    w, carry = _sb_block(nz, new_mask, _later_keys(page, 2), jnp.zeros((rows, 1), F32))
    acc = _dot(w.astype(BF16), v_new)

    per_block = 2 if n_pages % 2 == 0 else 1
    bw = per_block * page
    n_blk = n_pages // per_block
    all_pages = lambda refs: jnp.concatenate([r[...] for r in refs], axis=1).astype(BF16)
    nz = _dot(q_bd, all_pages(k_pages)) + nbias
    ls = _log_stay(nz, None)
    blocks = [slice(jb * bw, (jb + 1) * bw) for jb in range(n_blk)]
    later = _dot(jnp.concatenate([_prefix_operand(ls[:, b], 2) for b in blocks], axis=0), _later_keys(bw, 2))
    ws = [None] * n_blk
    for jb in reversed(range(n_blk)):
        b = blocks[jb]
        w, carry = _sb_weights(ls[:, b], nz[:, b], later[jb * rows:(jb + 1) * rows], carry, None)
        ws[jb] = w.astype(BF16)
    acc = acc + _dot_nt(jnp.concatenate(ws, axis=1), all_pages(v_pages))
    acc = jnp.where(same_head, acc, 0.0)
    out = acc[0:t_new, :]
    for h in range(1, N_HEADS):
        out = out + acc[h * t_new:(h + 1) * t_new, :]
    o_ref[...] = out.astype(BF16)


def _attn_sample(q, k_new, v_new, cache_k, cache_v, layer, page_table, bias_rows, t_new):
    n_seq, n_pages = page_table.shape
    page = cache_k.shape[3]
    page_spec = lambda j: pl.BlockSpec((None, None, ATT_WIDTH, page),
                                       lambda b, pt, j=j: (layer, pt[b, j], 0, 0))
    tok = pl.BlockSpec((t_new, ATT_WIDTH), lambda b, pt: (b, 0))
    seq_blk = pl.BlockSpec((None, t_new, ATT_WIDTH), lambda b, pt: (b, 0, 0))
    rows = N_HEADS * t_new
    whole = lambda w: pl.BlockSpec((rows, w), lambda b, pt: (0, 0))
    row_head = jnp.arange(rows, dtype=jnp.int32)[:, None] // t_new
    row_query = jnp.arange(rows, dtype=jnp.int32)[:, None] % t_new
    head_mask = (row_head == jnp.arange(ATT_WIDTH, dtype=jnp.int32)[None, :] // HEAD_DIM).astype(F32)
    new_mask = (jnp.arange(page, dtype=jnp.int32)[None, :] < row_query).astype(F32)
    kern = functools.partial(_attn_sample_kernel, n_pages=n_pages, t_new=t_new, page=page)
    grid_spec = pltpu.PrefetchScalarGridSpec(
        num_scalar_prefetch=1,
        grid=(n_seq,),
        in_specs=[whole(1), whole(ATT_WIDTH), whole(page), seq_blk, tok, tok]
                 + [page_spec(j) for j in range(n_pages)] * 2,
        out_specs=seq_blk,
    )
    o = pl.pallas_call(
        kern,
        grid_spec=grid_spec,
        out_shape=jax.ShapeDtypeStruct((n_seq, t_new, ATT_WIDTH), BF16),
        compiler_params=_params(("arbitrary",)),
        name="attn_sample",
    )(page_table, bias_rows, head_mask, new_mask, q.reshape(n_seq, t_new, ATT_WIDTH), k_new, v_new,
      *([cache_k] * n_pages), *([cache_v] * n_pages))
    return o.reshape(q.shape)


def _ssm_kernel(u_ref, d_ref, m_ref, wre_ref, wim_ref, vre_ref, vim_ref, a_ref, h0re_ref, h0im_ref,
                y_ref, hre_ref, him_ref, ug_s, yg_s, zre_s, zim_s, inre_s, inim_s, *, L, n_steps, bp):
    n_rows = n_steps * bp
    p = SSM_STATE
    gpb = GROUPS_PER_BLOCK
    slot = lax.shift_right_logical(lax.broadcasted_iota(jnp.int32, (1, LANES), 1), GROUP_CH.bit_length() - 1)

    def transpose_slots(a):
        a = list(a)
        k = gpb // 2
        while k:
            upper = jnp.bitwise_and(slot, k) != 0
            for i in range(gpb):
                if not i & k:
                    lo, hi = a[i], a[i + k]
                    a[i] = jnp.where(upper, pltpu.roll(hi, k * GROUP_CH, 1), lo)
                    a[i + k] = jnp.where(upper, hi, pltpu.roll(lo, LANES - k * GROUP_CH, 1))
            k //= 2
        return a

    rc = min(n_rows, SSM_ROW_CHUNK)
    assert n_rows % rc == 0 and L % gpb == 0
    n_half = L // gpb
    for r0 in range(0, n_rows, rc):
        for h in range(n_half):
            steps = [u_ref[pl.ds(r0 * L + h * gpb + j, rc, stride=L), :] for j in range(gpb)]
            for g, u_gh in enumerate(transpose_slots(steps)):
                ug_s[g, r0:r0 + rc, h * LANES:(h + 1) * LANES] = u_gh
    for g in range(gpb):
        u_g = ug_s[g]
        zre_s[:, g * p:(g + 1) * p] = _dot3(u_g, wre_ref[g])
        zim_s[:, g * p:(g + 1) * p] = _dot3(u_g, wim_ref[g])

    ar = a_ref[0:1, :]
    ai = a_ref[1:2, :]

    def step(k, carry):
        hr, hi = carry
        rows = pl.ds(k * bp, bp)
        inre_s[rows, :] = hr
        inim_s[rows, :] = hi
        zr = zre_s[rows, :]
        zi = zim_s[rows, :]
        return ar * hr - ai * hi + zr, ar * hi + ai * hr + zi

    hr, hi = lax.fori_loop(0, n_steps, step, (h0re_ref[...], h0im_ref[...]))
    hre_ref[...] = hr
    him_ref[...] = hi

    bdot = lambda a, b: _dot(a.astype(BF16), b.astype(BF16))
    for g in range(gpb):
        yg_s[g] = (bdot(ug_s[g], m_ref[g]) + bdot(inre_s[:, g * p:(g + 1) * p], vre_ref[g])
                   + bdot(inim_s[:, g * p:(g + 1) * p], vim_ref[g]))
    for r0 in range(0, n_rows, rc):
        for h in range(n_half):
            groups = [yg_s[g, r0:r0 + rc, h * LANES:(h + 1) * LANES] for g in range(gpb)]
            for j, y_step in enumerate(transpose_slots(groups)):
                rows = pl.ds(r0 * L + h * gpb + j, rc, stride=L)
                y_ref[rows, :] = y_step + d_ref[...] * u_ref[rows, :]


def _ssm_mats(a_re, a_im, log_dt, b_re, b_im, c_re, c_im, lengths):
    hp = lax.Precision.HIGHEST
    L = max(lengths)
    dt = jnp.exp(log_dt)[:, None]
    mag = jnp.exp(a_re * dt)
    ab_re, ab_im = mag * jnp.cos(a_im * dt), mag * jnp.sin(a_im * dt)
    den = a_re * a_re + a_im * a_im
    nr, ni = ab_re - 1.0, ab_im
    f_re = (nr * a_re + ni * a_im) / den
    f_im = (ni * a_re - nr * a_im) / den
    bb_re = f_re[..., None] * b_re - f_im[..., None] * b_im
    bb_im = f_re[..., None] * b_im + f_im[..., None] * b_re
    pr, pi = [jnp.ones_like(ab_re)], [jnp.zeros_like(ab_re)]
    for _ in range(L):
        pr, pi = pr + [ab_re * pr[-1] - ab_im * pi[-1]], pi + [ab_re * pi[-1] + ab_im * pr[-1]]
    p_re, p_im = jnp.stack(pr), jnp.stack(pi)
    abr = p_re[:L, :, :, None] * bb_re - p_im[:L, :, :, None] * bb_im
    abi = p_re[:L, :, :, None] * bb_im + p_im[:L, :, :, None] * bb_re
    kern = (jnp.einsum('gcp,tgpd->tgcd', c_re, abr, precision=hp)
            - jnp.einsum('gcp,tgpd->tgcd', c_im, abi, precision=hp))
    lc = L * GROUP_CH
    resp = jnp.transpose(kern, (1, 3, 0, 2)).reshape(N_GROUPS, GROUP_CH, lc)
    m = jnp.stack([jnp.pad(resp[:, :, :lc - s * GROUP_CH], ((0, 0), (0, 0), (s * GROUP_CH, 0)))
                   for s in range(L)], axis=1).reshape(N_GROUPS, lc, lc)
    idx = jnp.arange(L)
    flip = (idx[:, None] + idx[None, :] == L - 1).astype(F32)
    w_re = jnp.einsum('sz,zgpd->gsdp', flip, abr, precision=hp).reshape(N_GROUPS, lc, SSM_STATE)
    w_im = jnp.einsum('sz,zgpd->gsdp', flip, abi, precision=hp).reshape(N_GROUPS, lc, SSM_STATE)
    cr = jnp.transpose(c_re, (0, 2, 1))[:, :, None, :]
    ci = jnp.transpose(c_im, (0, 2, 1))[:, :, None, :]
    qr = jnp.transpose(p_re[1:], (1, 2, 0))[..., None]
    qi = jnp.transpose(p_im[1:], (1, 2, 0))[..., None]
    v_re = (cr * qr - ci * qi).reshape(N_GROUPS, SSM_STATE, lc)
    v_im = (-cr * qi - ci * qr).reshape(N_GROUPS, SSM_STATE, lc)

    def for_length(n):
        nc, skip = n * GROUP_CH, (L - n) * GROUP_CH
        a_n = jnp.stack([p_re[n], p_im[n]], axis=1)
        return (m[:, :nc, :nc], w_re[:, skip:], w_im[:, skip:], v_re[:, :, :nc], v_im[:, :, :nc], a_n)

    return {n: for_length(n) for n in lengths}


def _ssm(u, h0, mats, layer, d_skip, batch, seq, L):
    m, w_re, w_im, v_re, v_im, a_l = mats
    depth = m.shape[0]
    gpb = GROUPS_PER_BLOCK
    n_cb = N_GROUPS // gpb
    lc = L * GROUP_CH
    state_w = gpb * SSM_STATE
    if seq == L:
        n_bblk, bp, n_steps = 1, batch, 1
    else:
        n_bblk, bp, n_steps = batch, 1, seq // L
    tok = bp * n_steps * L
    n_rows = bp * n_steps

    def to_blocks(x):
        x = x.reshape(n_bblk, bp, n_cb, state_w)
        return jnp.transpose(x, (0, 2, 1, 3))

    if h0 is None:
        h0re = h0im = jnp.zeros((n_bblk, n_cb, bp, state_w), F32)
    else:
        h0re, h0im = to_blocks(h0[..., 0]), to_blocks(h0[..., 1])
    a_blk = jnp.transpose(a_l.reshape(depth, n_cb, gpb, 2, SSM_STATE), (0, 1, 3, 2, 4))
    a_blk = a_blk.reshape(depth, n_cb, 2, state_w)

    grp = lambda a, b: pl.BlockSpec((None, gpb, a, b), lambda i, cb: (layer, cb, 0, 0))
    act = pl.BlockSpec((tok, LANES), lambda i, cb: (i, cb))
    state = pl.BlockSpec((None, None, bp, state_w), lambda i, cb: (i, cb, 0, 0))
    state_shape = jax.ShapeDtypeStruct((n_bblk, n_cb, bp, state_w), F32)
    kern = functools.partial(_ssm_kernel, L=L, n_steps=n_steps, bp=bp)
    y, hre, him = pl.pallas_call(
        kern,
        grid=(n_bblk, n_cb),
        in_specs=[act, pl.BlockSpec((1, LANES), lambda i, cb: (0, cb)),
                  grp(lc, lc), grp(lc, SSM_STATE), grp(lc, SSM_STATE),
                  grp(SSM_STATE, lc), grp(SSM_STATE, lc),
                  pl.BlockSpec((None, None, 2, state_w), lambda i, cb: (layer, cb, 0, 0)), state, state],
        out_specs=[act, state, state],
        out_shape=[jax.ShapeDtypeStruct(u.shape, F32), state_shape, state_shape],
        scratch_shapes=[pltpu.VMEM((gpb, n_rows, lc), F32)] * 2 + [pltpu.VMEM((n_rows, state_w), F32)] * 4,
        compiler_params=_params(("parallel", "parallel")),
        name="ssm",
    )(u, d_skip.reshape(1, D_SSM), m, w_re, w_im, v_re, v_im, a_blk, h0re, h0im)

    def from_blocks(x):
        return jnp.transpose(x, (0, 2, 1, 3)).reshape(batch, N_GROUPS, SSM_STATE)

    return y, jnp.stack([from_blocks(hre), from_blocks(him)], axis=-1)


def _post_kernel(x_ref, o_ref, y_ref, ga_ref, gs_ref, p_ref,
                 wao_ref, wgv_ref, wgg_ref, wout_ref, gmlp_ref, wup_ref, wdown_ref,
                 gple_ref, wple_ref, wpg_ref, out_ref, *, ff_chunk):
    def rms(x, g_ref):
        ms = jnp.mean(x * x, axis=-1, keepdims=True)
        return (x * lax.rsqrt(ms + RMS_EPS) * g_ref[...]).astype(BF16)

    y_act = jax.nn.gelu(y_ref[...], approximate=True).astype(BF16)
    attn_out = _dot(o_ref[...], wao_ref[...])
    ssm_out = _dot(y_act, wgv_ref[...]) * jax.nn.sigmoid(_dot(y_act, wgg_ref[...]))
    merged = ga_ref[...].astype(F32) * attn_out + gs_ref[...].astype(F32) * ssm_out
    x = x_ref[...] + _dot(merged.astype(BF16), wout_ref[...])
    h2 = rms(x, gmlp_ref)
    d_ff = wup_ref.shape[1]
    mlp = None
    for c0 in range(0, d_ff, ff_chunk):
        hid = jnp.square(jnp.maximum(_dot(h2, wup_ref[:, c0:c0 + ff_chunk]), 0.0)).astype(BF16)
        part = _dot(hid, wdown_ref[c0:c0 + ff_chunk, :])
        mlp = part if mlp is None else mlp + part
    x = x + mlp
    h3 = rms(x, gple_ref)
    out_ref[...] = x + _dot(p_ref[...].astype(BF16), wple_ref[...]) * jax.nn.sigmoid(_dot(h3, wpg_ref[...]))


def _post(x, o, y, ga, gs, p, w):
    m, d = x.shape
    tm = min(ROW_TILE, m)
    row = lambda a: pl.BlockSpec((tm, a.shape[1]), lambda i: (i, 0))
    vec = lambda a: a.reshape(1, -1)
    acts = (x, o, y, ga, gs, p)
    consts = (w["w_attn_out"], w["w_glu_val"], w["w_glu_gate"], w["w_out"], vec(w["g_mlp"]),
              w["w_up"], w["w_down"], vec(w["g_ple"]), w["w_ple"], w["w_ple_gate"])
    return pl.pallas_call(
        functools.partial(_post_kernel, ff_chunk=FF_CHUNK),
        grid=(m // tm,),
        in_specs=[row(a) for a in acts] + [_resident(c.shape) for c in consts],
        out_specs=pl.BlockSpec((tm, d), lambda i: (i, 0)),
        out_shape=jax.ShapeDtypeStruct((m, d), F32),
        compiler_params=_params(("parallel",)),
        name="post",
    )(*acts, *consts)


def kernel(x_prompt, x_sample, p_prompt, p_sample, cache_k, cache_v, state_ssm, page_table, g_mix, w_in, g_q, g_k, sb_bias, w_attn_out, ssm_a_re, ssm_a_im, ssm_log_dt, ssm_b_re, ssm_b_im, ssm_c_re, ssm_c_im, ssm_d, w_glu_val, w_glu_gate, w_out, g_mlp, w_up, w_down, g_ple, w_ple, w_ple_gate):
    depth = w_in.shape[0]
    batch, seq, d_model = x_prompt.shape
    dec_batch, dec_seq, _ = x_sample.shape
    m_p, m_s = batch * seq, dec_batch * dec_seq
    to_pages = lambda c: jnp.transpose(c, (0, 1, 3, 4, 2)).reshape(c.shape[:2] + (ATT_WIDTH, c.shape[2]))
    ck, cv = to_pages(cache_k), to_pages(cache_v)
    hd = jnp.arange(ATT_WIDTH, dtype=jnp.int32) // HEAD_DIM
    bd = jnp.where(hd[:, None] == hd[None, :], 1.0 / HEAD_DIM, 0.0).astype(BF16)

    mats = jax.vmap(lambda *p: _ssm_mats(*p, sorted({PROMPT_CHUNK, dec_seq})))(
        ssm_a_re, ssm_a_im, ssm_log_dt, ssm_b_re, ssm_b_im, ssm_c_re, ssm_c_im)

    xp, xs = x_prompt.reshape(m_p, d_model), x_sample.reshape(m_s, d_model)
    kt_all = vt_all = None
    outs = [[] for _ in range(4)]
    for i in range(depth):
        w_in_bf = w_in[i].astype(BF16)
        w = {
            "g_mix": g_mix[i], "w_in": w_in_bf, "bd": bd,
            "w_kvt": w_in_bf[:, ATT_WIDTH:3 * ATT_WIDTH].T,
            "g_q": jnp.tile(g_q[i], N_HEADS).reshape(1, ATT_WIDTH),
            "g_k": jnp.tile(g_k[i], N_HEADS).reshape(1, ATT_WIDTH),
            "w_attn_out": w_attn_out[i].astype(BF16), "w_glu_val": w_glu_val[i].astype(BF16),
            "w_glu_gate": w_glu_gate[i].astype(BF16), "w_out": w_out[i].astype(BF16),
            "g_mlp": g_mlp[i], "w_up": w_up[i].astype(BF16), "w_down": w_down[i].astype(BF16),
            "g_ple": g_ple[i], "w_ple": w_ple[i].astype(BF16), "w_ple_gate": w_ple_gate[i].astype(BF16),
        }
        bias = sb_bias[i].astype(F32)
        bias_rows = jnp.repeat(bias, dec_seq).reshape(N_HEADS * dec_seq, 1)

        q, kt_all, vt_all, u, ga, gs = _in_proj(xp, w, (i, depth, batch, seq, kt_all, vt_all))
        o = _attn_prompt(q, kt_all, vt_all, i, bias, batch, seq)
        y, h_p = _ssm(u, None, mats[PROMPT_CHUNK], i, ssm_d[i], batch, seq, PROMPT_CHUNK)
        xp = _post(xp, o, y, ga, gs, p_prompt[i].reshape(m_p, -1), w)

        q, k_s, v_s, u, ga, gs = _in_proj(xs, w)
        o = _attn_sample(q, k_s, v_s, ck, cv, i, page_table, bias_rows, dec_seq)
        y, h_s = _ssm(u, state_ssm[i], mats[dec_seq], i, ssm_d[i], dec_batch, dec_seq, dec_seq)
        xs = _post(xs, o, y, ga, gs, p_sample[i].reshape(m_s, -1), w)

        shape4 = (dec_batch, dec_seq, N_HEADS, HEAD_DIM)
        for lst, val in zip(outs, (h_p, k_s.reshape(shape4), v_s.reshape(shape4), h_s)):
            lst.append(val)
    s_p, k_s, v_s, s_s = (jnp.stack(o) for o in outs)
    from_t = lambda t: jnp.transpose(t.reshape(depth, batch, N_HEADS, HEAD_DIM, seq), (0, 1, 4, 2, 3))
    return (xp.reshape(x_prompt.shape), xs.reshape(x_sample.shape), from_t(kt_all), from_t(vt_all), s_p,
            k_s, v_s, s_s)
```

```python
import functools

import jax
import jax.numpy as jnp
from jax import lax
from jax.experimental import pallas as pl
from jax.experimental.pallas import tpu as pltpu

F32 = jnp.float32
BF16 = jnp.bfloat16

N_HEADS = 8
HEAD_DIM = 64
ATT_WIDTH = N_HEADS * HEAD_DIM
GROUP_CH = 16
N_GROUPS = 32
SSM_STATE = 64
D_SSM = N_GROUPS * GROUP_CH
RMS_EPS = 1e-6
LOG2E = 1.4426950408889634
Q_SCALE = -(HEAD_DIM ** -0.5) * LOG2E
LANES = 128
MXU_TILE = 256
GROUPS_PER_BLOCK = LANES // GROUP_CH
VMEM_LIMIT = 56 * 1024 * 1024

PROMPT_CHUNK = 16
SSM_ROW_CHUNK = 64
ATT_TK = 256
ATT_TQ = 2 * ATT_TK
ROW_TILE = 256
FF_CHUNK = 1024


def _dot(a, b):
    return jnp.dot(a, b, preferred_element_type=F32)


def _dot_nt(a, b):
    return lax.dot_general(a, b, (((1,), (1,)), ((), ())), preferred_element_type=F32)


def _split(x):
    hi = x.astype(BF16)
    lo = (x - hi.astype(F32)).astype(BF16)
    return hi, lo


def _dot3(a, b):
    ah, al = _split(a)
    bh, bl = _split(b)
    return _dot(ah, bh) + _dot(al, bh) + _dot(ah, bl)


def _params(sem):
    return pltpu.CompilerParams(dimension_semantics=sem, vmem_limit_bytes=VMEM_LIMIT)


def _resident(shape):
    nd = len(shape)
    return pl.BlockSpec(shape, lambda *_: (0,) * nd, pipeline_mode=pl.Buffered(1))


def _in_proj_kernel(x_ref, gmix_ref, w_ref, bd_ref, gq_ref, gk_ref, *rest, kv_transposed, n_prev):
    x = x_ref[...]
    ms = jnp.mean(x * x, axis=-1, keepdims=True)
    h = (x * lax.rsqrt(ms + RMS_EPS) * gmix_ref[...]).astype(BF16)
    a = ATT_WIDTH

    def head_norm(p, g_ref):
        msh = _dot((p * p).astype(BF16), bd_ref[...])
        return p * lax.rsqrt(msh + RMS_EPS) * g_ref[...]

    if kv_transposed:
        wkvt_ref = rest[0]
        q_ref, k_ref, v_ref, u_ref, ga_ref, gs_ref = rest[1 + n_prev:]
        kvt = jnp.concatenate([_dot_nt(wkvt_ref[r:r + MXU_TILE, :], h) for r in range(0, 2 * a, MXU_TILE)], axis=0)
        kt = kvt[:a]
        mst = _dot(bd_ref[...], (kt * kt).astype(BF16))
        k_ref[...] = kt * lax.rsqrt(mst + RMS_EPS) * gk_ref[...]
        v_ref[...] = kvt[a:]
    else:
        q_ref, k_ref, v_ref, u_ref, ga_ref, gs_ref = rest
        k_ref[...] = head_norm(_dot(h, w_ref[:, a:2 * a]), gk_ref)
        v_ref[...] = _dot(h, w_ref[:, 2 * a:3 * a])
    q_ref[...] = (head_norm(_dot(h, w_ref[:, 0:a]), gq_ref) * Q_SCALE).astype(BF16)
    u_ref[...] = _dot(h, w_ref[:, 3 * a:3 * a + D_SSM])
    o = 3 * a + D_SSM
    d = ga_ref.shape[-1]
    ga_ref[...] = jax.nn.sigmoid(_dot(h, w_ref[:, o:o + d])).astype(BF16)
    gs_ref[...] = jax.nn.sigmoid(_dot(h, w_ref[:, o + d:o + 2 * d])).astype(BF16)


def _in_proj(x, w, kv_slot=None):
    m, d = x.shape
    tm = min(ROW_TILE, m)
    row = lambda width: pl.BlockSpec((tm, width), lambda i: (i, 0))
    tok = lambda width, t: (row(width), jax.ShapeDtypeStruct((m, width), t))
    ins = [x, w["g_mix"].reshape(1, d), w["w_in"], w["bd"], w["g_q"]]
    in_specs = [row(d), _resident((1, d)), _resident(w["w_in"].shape), _resident(w["bd"].shape),
                _resident((1, ATT_WIDTH))]
    aliases = {}
    n_prev = 0
    if kv_slot is None:
        ins.append(w["g_k"])
        in_specs.append(_resident((1, ATT_WIDTH)))
        kv = [tok(ATT_WIDTH, F32)] * 2
    else:
        layer, depth, batch, seq, prev_kt, prev_vt = kv_slot
        per_seq = seq // tm
        ins += [w["g_k"].reshape(ATT_WIDTH, 1), w["w_kvt"]]
        in_specs += [_resident((ATT_WIDTH, 1)), _resident(w["w_kvt"].shape)]
        if prev_kt is not None:
            n_prev = 2
            aliases = {len(ins): 1, len(ins) + 1: 2}
            ins += [prev_kt, prev_vt]
            in_specs += [pl.BlockSpec(memory_space=pl.ANY)] * 2
        slot = pl.BlockSpec((None, None, ATT_WIDTH, tm), lambda i: (layer, i // per_seq, 0, i % per_seq))
        kv = [(slot, jax.ShapeDtypeStruct((depth, batch, ATT_WIDTH, seq), F32))] * 2
    outs = [tok(ATT_WIDTH, BF16)] + kv + [tok(D_SSM, F32), tok(d, BF16), tok(d, BF16)]
    return pl.pallas_call(
        functools.partial(_in_proj_kernel, kv_transposed=kv_slot is not None, n_prev=n_prev),
        grid=(m // tm,),
        in_specs=in_specs,
        out_specs=[o[0] for o in outs],
        out_shape=[o[1] for o in outs],
        input_output_aliases=aliases,
        compiler_params=_params(("parallel",)),
        name="in_proj",
    )(*ins)


def _log_stay(nz, mask):
    m = jnp.minimum(nz, 0.0)
    ls = m - jnp.log(1.0 + jnp.exp2(m + (m - nz))) * LOG2E
    return ls if mask is None else jnp.where(mask, ls, 0.0)


def _prefix_operand(ls, copies):
    return ls.astype(BF16) if copies == 1 else jnp.concatenate(_split(ls), axis=1)


def _sb_weights(ls, nz, later, carry, mask):
    after = later + carry
    w = jnp.exp2((ls - nz) + after)
    if mask is not None:
        w = jnp.where(mask, w, 0.0)
    return w, after[:, 0:1] + ls[:, 0:1]


def _sb_block(nz, mask, tri, carry):
    ls = _log_stay(nz, mask)
    later = _dot(_prefix_operand(ls, tri.shape[0] // tri.shape[1]), tri)
    return _sb_weights(ls, nz, later, carry, mask)


def _later_keys(n, copies):
    assert copies in (1, 2)
    r = lax.broadcasted_iota(jnp.int32, (copies * n, n), 0)
    c = lax.broadcasted_iota(jnp.int32, (copies * n, n), 1)
    return jnp.where(jnp.where(r >= n, r - n, r) > c, 1.0, 0.0).astype(BF16)


def _attn_prompt_kernel(bias_ref, q_ref, kt_ref, vt_ref, o_ref, kb_s, vb_s, acc_ref, carry_ref,
                        *, tq, tk, seq):
    hp = pl.program_id(1)
    qi = pl.program_id(2)
    lane = lax.broadcasted_iota(jnp.int32, (1, LANES), 1)
    first_head = lane < HEAD_DIM

    @pl.when(qi == 0)
    def _stage_keys_values():
        ones = jnp.where(lax.broadcasted_iota(jnp.int32, (LANES, tk), 0) < 2, 1.0, 0.0)
        for i in range(seq // tk):
            kb_s[i] = jnp.concatenate([kt_ref[:, i * tk:(i + 1) * tk], ones], axis=0).astype(BF16)
            vb_s[i] = vt_ref[:, i * tk:(i + 1) * tk].astype(BF16)

    qs = q_ref[...]
    q2 = jnp.concatenate([jnp.where(first_head, qs, 0), jnp.where(first_head, 0, qs)], axis=0)

    def bias_lanes(h):
        b = jnp.full((tq, LANES), bias_ref[2 * hp + h] * (-LOG2E), F32)
        b_hi = b.astype(BF16).astype(F32)
        return jnp.where(lane == 0, b_hi, jnp.where(lane == 1, b - b_hi, 0.0)).astype(BF16)
    q2 = jnp.concatenate([q2, jnp.concatenate([bias_lanes(0), bias_lanes(1)], axis=0)], axis=1)
    tri = _later_keys(tk, 1)

    def block(kb, mask, carry, q_rows):
        n = q_rows.shape[0] // 2
        nz = _dot(q_rows, kb_s[kb])
        w, carry = _sb_block(nz, mask, tri, carry)
        pv = _dot_nt(w.astype(BF16), vb_s[kb])
        return jnp.where(first_head, pv[:n], pv[n:]), carry

    assert tq == 2 * tk
    top = 2 * qi + 1
    r = lax.broadcasted_iota(jnp.int32, (2 * tk, tk), 0)
    c = lax.broadcasted_iota(jnp.int32, (2 * tk, tk), 1)
    triangle = c < jnp.where(r < tk, r, r - tk)
    half = lambda x, lo: jnp.concatenate([x[lo:lo + tk], x[tq + lo:tq + lo + tk]], axis=0)
    q_early, q_late = half(q2, 0), half(q2, tk)
    zero = jnp.zeros((2 * tk, 1), F32)
    late_top, c_late = block(top, triangle, zero, q_late)
    late_full, c_late = block(top - 1, None, c_late, q_late)
    early, c_early = block(top - 1, triangle, zero, q_early)
    acc_ref[0:tk, :] = early
    acc_ref[tk:tq, :] = late_top + late_full
    carry_ref[...] = jnp.concatenate([c_early[:tk], c_late[:tk], c_early[tk:], c_late[tk:]], axis=0)

    def below(first, n):
        carry = carry_ref[...]
        total = None
        for d in range(n):
            contrib, carry = block(first - d, None, carry, q2)
            total = contrib if total is None else total + contrib
        acc_ref[...] += total
        carry_ref[...] = carry

    odd = jnp.bitwise_and(qi, 1)

    @pl.when(odd == 1)
    def _pair():
        below(top - 2, 2)

    def four(i, _):
        below(top - 2 - 2 * odd - 4 * i, 4)
        return 0
    lax.fori_loop(0, lax.shift_right_logical(qi, 1), four, 0)
    o_ref[...] = acc_ref[...].astype(BF16)


def _attn_prompt(q, kt_all, vt_all, layer, bias, batch, seq):
    tq, tk = ATT_TQ, ATT_TK
    nq = seq // tq
    kern = functools.partial(_attn_prompt_kernel, tq=tq, tk=tk, seq=seq)
    head_pair = pl.BlockSpec((None, None, LANES, seq), lambda b, hp, i: (layer, b, hp, 0))
    return pl.pallas_call(
        kern,
        grid=(batch, N_HEADS // 2, nq),
        in_specs=[pl.BlockSpec(memory_space=pltpu.SMEM),
                  pl.BlockSpec((tq, LANES), lambda b, hp, i: (b * nq + i, hp)),
                  head_pair, head_pair],
        out_specs=pl.BlockSpec((tq, LANES), lambda b, hp, i: (b * nq + i, hp)),
        out_shape=jax.ShapeDtypeStruct(q.shape, BF16),
        scratch_shapes=[pltpu.VMEM((seq // tk, 2 * LANES, tk), BF16), pltpu.VMEM((seq // tk, LANES, tk), BF16)]
                       + [pltpu.VMEM((tq, LANES), F32), pltpu.VMEM((2 * tq, 1), F32)],
        compiler_params=_params(("parallel", "parallel", "arbitrary")),
        name="attn_prompt",
    )(bias, q, kt_all, vt_all)


def _attn_sample_kernel(pt_ref, bias_ref, hmask_ref, nmask_ref, q_ref, kn_ref, vn_ref, *rest,
                        n_pages, t_new, page):
    del pt_ref
    k_pages = rest[:n_pages]
    v_pages = rest[n_pages:2 * n_pages]
    o_ref = rest[2 * n_pages]
    rows = N_HEADS * t_new
    same_head = hmask_ref[...] > 0.5
    q = q_ref[...].astype(F32)
    q_bd = jnp.where(same_head, jnp.concatenate([q] * N_HEADS, axis=0), 0.0).astype(BF16)
    nbias = bias_ref[...] * (-LOG2E)

    pad = jnp.zeros((page - t_new, ATT_WIDTH), F32)
    k_new = jnp.concatenate([kn_ref[...], pad], axis=0).astype(BF16)
    v_new = jnp.concatenate([vn_ref[...], pad], axis=0).astype(BF16)
    new_mask = nmask_ref[...] > 0.5
    nz = _dot_nt(q_bd, k_new) + nbias
    w, carry = _sb_block(nz, new_mask, _later_keys(page, 2), jnp.zeros((rows, 1), F32))
    acc = _dot(w.astype(BF16), v_new)

    per_block = 2 if n_pages % 2 == 0 else 1
    bw = per_block * page
    n_blk = n_pages // per_block
    all_pages = lambda refs: jnp.concatenate([r[...] for r in refs], axis=1).astype(BF16)
    nz = _dot(q_bd, all_pages(k_pages)) + nbias
    ls = _log_stay(nz, None)
    blocks = [slice(jb * bw, (jb + 1) * bw) for jb in range(n_blk)]
    later = _dot(jnp.concatenate([_prefix_operand(ls[:, b], 2) for b in blocks], axis=0), _later_keys(bw, 2))
    laters = [later[jb * rows:(jb + 1) * rows] for jb in range(n_blk)]
    sums = [laters[jb][:, 0:1] + ls[:, blocks[jb]][:, 0:1] for jb in range(n_blk)]
    ws = [None] * n_blk
    for jb in reversed(range(n_blk)):
        b = blocks[jb]
        ws[jb] = _sb_weights(ls[:, b], nz[:, b], laters[jb], carry, None)[0].astype(BF16)
        carry = carry + sums[jb]
    acc = acc + _dot_nt(jnp.concatenate(ws, axis=1), all_pages(v_pages))
    acc = jnp.where(same_head, acc, 0.0)
    out = acc[0:t_new, :]
    for h in range(1, N_HEADS):
        out = out + acc[h * t_new:(h + 1) * t_new, :]
    o_ref[...] = out.astype(BF16)


def _attn_sample(q, k_new, v_new, cache_k, cache_v, layer, page_table, bias_rows, t_new):
    n_seq, n_pages = page_table.shape
    page = cache_k.shape[3]
    page_spec = lambda j: pl.BlockSpec((None, None, ATT_WIDTH, page),
                                       lambda b, pt, j=j: (layer, pt[b, j], 0, 0))
    tok = pl.BlockSpec((t_new, ATT_WIDTH), lambda b, pt: (b, 0))
    seq_blk = pl.BlockSpec((None, t_new, ATT_WIDTH), lambda b, pt: (b, 0, 0))
    rows = N_HEADS * t_new
    whole = lambda w: pl.BlockSpec((rows, w), lambda b, pt: (0, 0))
    row_head = jnp.arange(rows, dtype=jnp.int32)[:, None] // t_new
    row_query = jnp.arange(rows, dtype=jnp.int32)[:, None] % t_new
    head_mask = (row_head == jnp.arange(ATT_WIDTH, dtype=jnp.int32)[None, :] // HEAD_DIM).astype(F32)
    new_mask = (jnp.arange(page, dtype=jnp.int32)[None, :] < row_query).astype(F32)
    kern = functools.partial(_attn_sample_kernel, n_pages=n_pages, t_new=t_new, page=page)
    grid_spec = pltpu.PrefetchScalarGridSpec(
        num_scalar_prefetch=1,
        grid=(n_seq,),
        in_specs=[whole(1), whole(ATT_WIDTH), whole(page), seq_blk, tok, tok]
                 + [page_spec(j) for j in range(n_pages)] * 2,
        out_specs=seq_blk,
    )
    o = pl.pallas_call(
        kern,
        grid_spec=grid_spec,
        out_shape=jax.ShapeDtypeStruct((n_seq, t_new, ATT_WIDTH), BF16),
        compiler_params=_params(("arbitrary",)),
        name="attn_sample",
    )(page_table, bias_rows, head_mask, new_mask, q.reshape(n_seq, t_new, ATT_WIDTH), k_new, v_new,
      *([cache_k] * n_pages), *([cache_v] * n_pages))
    return o.reshape(q.shape)


def _ssm_kernel(u_ref, d_ref, m_ref, wre_ref, wim_ref, vre_ref, vim_ref, a_ref, h0re_ref, h0im_ref,
                y_ref, hre_ref, him_ref, ug_s, yg_s, zre_s, zim_s, inre_s, inim_s, *, L, n_steps, bp):
    n_rows = n_steps * bp
    p = SSM_STATE
    gpb = GROUPS_PER_BLOCK
    slot = lax.shift_right_logical(lax.broadcasted_iota(jnp.int32, (1, LANES), 1), GROUP_CH.bit_length() - 1)

    def transpose_slots(a):
        a = list(a)
        k = gpb // 2
        while k:
            upper = jnp.bitwise_and(slot, k) != 0
            for i in range(gpb):
                if not i & k:
                    lo, hi = a[i], a[i + k]
                    a[i] = jnp.where(upper, pltpu.roll(hi, k * GROUP_CH, 1), lo)
                    a[i + k] = jnp.where(upper, hi, pltpu.roll(lo, LANES - k * GROUP_CH, 1))
            k //= 2
        return a

    rc = min(n_rows, SSM_ROW_CHUNK)
    assert n_rows % rc == 0 and L % gpb == 0
    n_half = L // gpb
    for r0 in range(0, n_rows, rc):
        for h in range(n_half):
            steps = [u_ref[pl.ds(r0 * L + h * gpb + j, rc, stride=L), :] for j in range(gpb)]
            for g, u_gh in enumerate(transpose_slots(steps)):
                ug_s[g, r0:r0 + rc, h * LANES:(h + 1) * LANES] = u_gh
    for g in range(gpb):
        u_g = ug_s[g]
        zre_s[:, g * p:(g + 1) * p] = _dot3(u_g, wre_ref[g])
        zim_s[:, g * p:(g + 1) * p] = _dot3(u_g, wim_ref[g])

    ar = a_ref[0:1, :]
    ai = a_ref[1:2, :]

    def step(k, carry):
        hr, hi = carry
        rows = pl.ds(k * bp, bp)
        inre_s[rows, :] = hr
        inim_s[rows, :] = hi
        zr = zre_s[rows, :]
        zi = zim_s[rows, :]
        return ar * hr - ai * hi + zr, ar * hi + ai * hr + zi

    hr, hi = lax.fori_loop(0, n_steps, step, (h0re_ref[...], h0im_ref[...]))
    hre_ref[...] = hr
    him_ref[...] = hi

    bdot = lambda a, b: _dot(a.astype(BF16), b.astype(BF16))
    for g in range(gpb):
        yg_s[g] = (bdot(ug_s[g], m_ref[g]) + bdot(inre_s[:, g * p:(g + 1) * p], vre_ref[g])
                   + bdot(inim_s[:, g * p:(g + 1) * p], vim_ref[g]))
    for r0 in range(0, n_rows, rc):
        for h in range(n_half):
            groups = [yg_s[g, r0:r0 + rc, h * LANES:(h + 1) * LANES] for g in range(gpb)]
            for j, y_step in enumerate(transpose_slots(groups)):
                rows = pl.ds(r0 * L + h * gpb + j, rc, stride=L)
                y_ref[rows, :] = y_step + d_ref[...] * u_ref[rows, :]


def _ssm_mats(a_re, a_im, log_dt, b_re, b_im, c_re, c_im, lengths):
    hp = lax.Precision.HIGHEST
    L = max(lengths)
    dt = jnp.exp(log_dt)[:, None]
    mag = jnp.exp(a_re * dt)
    ab_re, ab_im = mag * jnp.cos(a_im * dt), mag * jnp.sin(a_im * dt)
    den = a_re * a_re + a_im * a_im
    nr, ni = ab_re - 1.0, ab_im
    f_re = (nr * a_re + ni * a_im) / den
    f_im = (ni * a_re - nr * a_im) / den
    bb_re = f_re[..., None] * b_re - f_im[..., None] * b_im
    bb_im = f_re[..., None] * b_im + f_im[..., None] * b_re
    pr, pi = [jnp.ones_like(ab_re)], [jnp.zeros_like(ab_re)]
    for _ in range(L):
        pr, pi = pr + [ab_re * pr[-1] - ab_im * pi[-1]], pi + [ab_re * pi[-1] + ab_im * pr[-1]]
    p_re, p_im = jnp.stack(pr), jnp.stack(pi)
    abr = p_re[:L, :, :, None] * bb_re - p_im[:L, :, :, None] * bb_im
    abi = p_re[:L, :, :, None] * bb_im + p_im[:L, :, :, None] * bb_re
    kern = (jnp.einsum('gcp,tgpd->tgcd', c_re, abr, precision=hp)
            - jnp.einsum('gcp,tgpd->tgcd', c_im, abi, precision=hp))
    lc = L * GROUP_CH
    resp = jnp.transpose(kern, (1, 3, 0, 2)).reshape(N_GROUPS, GROUP_CH, lc)
    m = jnp.stack([jnp.pad(resp[:, :, :lc - s * GROUP_CH], ((0, 0), (0, 0), (s * GROUP_CH, 0)))
                   for s in range(L)], axis=1).reshape(N_GROUPS, lc, lc)
    idx = jnp.arange(L)
    flip = (idx[:, None] + idx[None, :] == L - 1).astype(F32)
    w_re = jnp.einsum('sz,zgpd->gsdp', flip, abr, precision=hp).reshape(N_GROUPS, lc, SSM_STATE)
    w_im = jnp.einsum('sz,zgpd->gsdp', flip, abi, precision=hp).reshape(N_GROUPS, lc, SSM_STATE)
    cr = jnp.transpose(c_re, (0, 2, 1))[:, :, None, :]
    ci = jnp.transpose(c_im, (0, 2, 1))[:, :, None, :]
    qr = jnp.transpose(p_re[1:], (1, 2, 0))[..., None]
    qi = jnp.transpose(p_im[1:], (1, 2, 0))[..., None]
    v_re = (cr * qr - ci * qi).reshape(N_GROUPS, SSM_STATE, lc)
    v_im = (-cr * qi - ci * qr).reshape(N_GROUPS, SSM_STATE, lc)

    def for_length(n):
        nc, skip = n * GROUP_CH, (L - n) * GROUP_CH
        a_n = jnp.stack([p_re[n], p_im[n]], axis=1)
        return (m[:, :nc, :nc], w_re[:, skip:], w_im[:, skip:], v_re[:, :, :nc], v_im[:, :, :nc], a_n)

    return {n: for_length(n) for n in lengths}


def _ssm(u, h0, mats, layer, d_skip, batch, seq, L):
    m, w_re, w_im, v_re, v_im, a_l = mats
    depth = m.shape[0]
    gpb = GROUPS_PER_BLOCK
    n_cb = N_GROUPS // gpb
    lc = L * GROUP_CH
    state_w = gpb * SSM_STATE
    if seq == L:
        n_bblk, bp, n_steps = 1, batch, 1
    else:
        n_bblk, bp, n_steps = batch, 1, seq // L
    tok = bp * n_steps * L
    n_rows = bp * n_steps

    def to_blocks(x):
        x = x.reshape(n_bblk, bp, n_cb, state_w)
        return jnp.transpose(x, (0, 2, 1, 3))

    if h0 is None:
        h0re = h0im = jnp.zeros((n_bblk, n_cb, bp, state_w), F32)
    else:
        h0re, h0im = to_blocks(h0[..., 0]), to_blocks(h0[..., 1])
    a_blk = jnp.transpose(a_l.reshape(depth, n_cb, gpb, 2, SSM_STATE), (0, 1, 3, 2, 4))
    a_blk = a_blk.reshape(depth, n_cb, 2, state_w)

    grp = lambda a, b: pl.BlockSpec((None, gpb, a, b), lambda i, cb: (layer, cb, 0, 0))
    act = pl.BlockSpec((tok, LANES), lambda i, cb: (i, cb))
    state = pl.BlockSpec((None, None, bp, state_w), lambda i, cb: (i, cb, 0, 0))
    state_shape = jax.ShapeDtypeStruct((n_bblk, n_cb, bp, state_w), F32)
    kern = functools.partial(_ssm_kernel, L=L, n_steps=n_steps, bp=bp)
    y, hre, him = pl.pallas_call(
        kern,
        grid=(n_bblk, n_cb),
        in_specs=[act, pl.BlockSpec((1, LANES), lambda i, cb: (0, cb)),
                  grp(lc, lc), grp(lc, SSM_STATE), grp(lc, SSM_STATE),
                  grp(SSM_STATE, lc), grp(SSM_STATE, lc),
                  pl.BlockSpec((None, None, 2, state_w), lambda i, cb: (layer, cb, 0, 0)), state, state],
        out_specs=[act, state, state],
        out_shape=[jax.ShapeDtypeStruct(u.shape, F32), state_shape, state_shape],
        scratch_shapes=[pltpu.VMEM((gpb, n_rows, lc), F32)] * 2 + [pltpu.VMEM((n_rows, state_w), F32)] * 4,
        compiler_params=_params(("parallel", "parallel")),
        name="ssm",
    )(u, d_skip.reshape(1, D_SSM), m, w_re, w_im, v_re, v_im, a_blk, h0re, h0im)

    def from_blocks(x):
        return jnp.transpose(x, (0, 2, 1, 3)).reshape(batch, N_GROUPS, SSM_STATE)

    return y, jnp.stack([from_blocks(hre), from_blocks(him)], axis=-1)


def _post_kernel(x_ref, o_ref, y_ref, ga_ref, gs_ref, p_ref,
                 wao_ref, wgv_ref, wgg_ref, wout_ref, gmlp_ref, wup_ref, wdown_ref,
                 gple_ref, wple_ref, wpg_ref, out_ref, *, ff_chunk):
    def rms(x, g_ref):
        ms = jnp.mean(x * x, axis=-1, keepdims=True)
        return (x * lax.rsqrt(ms + RMS_EPS) * g_ref[...]).astype(BF16)

    y_act = jax.nn.gelu(y_ref[...], approximate=True).astype(BF16)
    attn_out = _dot(o_ref[...], wao_ref[...])
    ssm_out = _dot(y_act, wgv_ref[...]) * jax.nn.sigmoid(_dot(y_act, wgg_ref[...]))
    merged = ga_ref[...].astype(F32) * attn_out + gs_ref[...].astype(F32) * ssm_out
    x = x_ref[...] + _dot(merged.astype(BF16), wout_ref[...])
    h2 = rms(x, gmlp_ref)
    d_ff = wup_ref.shape[1]
    mlp = None
    for c0 in range(0, d_ff, ff_chunk):
        hid = jnp.square(jnp.maximum(_dot(h2, wup_ref[:, c0:c0 + ff_chunk]), 0.0)).astype(BF16)
        part = _dot(hid, wdown_ref[c0:c0 + ff_chunk, :])
        mlp = part if mlp is None else mlp + part
    x = x + mlp
    h3 = rms(x, gple_ref)
    out_ref[...] = x + _dot(p_ref[...].astype(BF16), wple_ref[...]) * jax.nn.sigmoid(_dot(h3, wpg_ref[...]))


def _post(x, o, y, ga, gs, p, w):
    m, d = x.shape
    tm = min(ROW_TILE, m)
    row = lambda a: pl.BlockSpec((tm, a.shape[1]), lambda i: (i, 0))
    vec = lambda a: a.reshape(1, -1)
    acts = (x, o, y, ga, gs, p)
    consts = (w["w_attn_out"], w["w_glu_val"], w["w_glu_gate"], w["w_out"], vec(w["g_mlp"]),
              w["w_up"], w["w_down"], vec(w["g_ple"]), w["w_ple"], w["w_ple_gate"])
    return pl.pallas_call(
        functools.partial(_post_kernel, ff_chunk=FF_CHUNK),
        grid=(m // tm,),
        in_specs=[row(a) for a in acts] + [_resident(c.shape) for c in consts],
        out_specs=pl.BlockSpec((tm, d), lambda i: (i, 0)),
        out_shape=jax.ShapeDtypeStruct((m, d), F32),
        compiler_params=_params(("parallel",)),
        name="post",
    )(*acts, *consts)


def kernel(x_prompt, x_sample, p_prompt, p_sample, cache_k, cache_v, state_ssm, page_table, g_mix, w_in, g_q, g_k, sb_bias, w_attn_out, ssm_a_re, ssm_a_im, ssm_log_dt, ssm_b_re, ssm_b_im, ssm_c_re, ssm_c_im, ssm_d, w_glu_val, w_glu_gate, w_out, g_mlp, w_up, w_down, g_ple, w_ple, w_ple_gate):
    depth = w_in.shape[0]
    batch, seq, d_model = x_prompt.shape
    dec_batch, dec_seq, _ = x_sample.shape
    m_p, m_s = batch * seq, dec_batch * dec_seq
    to_pages = lambda c: jnp.transpose(c, (0, 1, 3, 4, 2)).reshape(c.shape[:2] + (ATT_WIDTH, c.shape[2]))
    ck, cv = to_pages(cache_k), to_pages(cache_v)
    hd = jnp.arange(ATT_WIDTH, dtype=jnp.int32) // HEAD_DIM
    bd = jnp.where(hd[:, None] == hd[None, :], 1.0 / HEAD_DIM, 0.0).astype(BF16)

    mats = jax.vmap(lambda *p: _ssm_mats(*p, sorted({PROMPT_CHUNK, dec_seq})))(
        ssm_a_re, ssm_a_im, ssm_log_dt, ssm_b_re, ssm_b_im, ssm_c_re, ssm_c_im)

    xp, xs = x_prompt.reshape(m_p, d_model), x_sample.reshape(m_s, d_model)
    kt_all = vt_all = None
    outs = [[] for _ in range(4)]
    for i in range(depth):
        w_in_bf = w_in[i].astype(BF16)
        w = {
            "g_mix": g_mix[i], "w_in": w_in_bf, "bd": bd,
            "w_kvt": w_in_bf[:, ATT_WIDTH:3 * ATT_WIDTH].T,
            "g_q": jnp.tile(g_q[i], N_HEADS).reshape(1, ATT_WIDTH),
            "g_k": jnp.tile(g_k[i], N_HEADS).reshape(1, ATT_WIDTH),
            "w_attn_out": w_attn_out[i].astype(BF16), "w_glu_val": w_glu_val[i].astype(BF16),
            "w_glu_gate": w_glu_gate[i].astype(BF16), "w_out": w_out[i].astype(BF16),
            "g_mlp": g_mlp[i], "w_up": w_up[i].astype(BF16), "w_down": w_down[i].astype(BF16),
            "g_ple": g_ple[i], "w_ple": w_ple[i].astype(BF16), "w_ple_gate": w_ple_gate[i].astype(BF16),
        }
        bias = sb_bias[i].astype(F32)
        bias_rows = jnp.repeat(bias, dec_seq).reshape(N_HEADS * dec_seq, 1)

        q, kt_all, vt_all, u, ga, gs = _in_proj(xp, w, (i, depth, batch, seq, kt_all, vt_all))
        o = _attn_prompt(q, kt_all, vt_all, i, bias, batch, seq)
        y, h_p = _ssm(u, None, mats[PROMPT_CHUNK], i, ssm_d[i], batch, seq, PROMPT_CHUNK)
        xp = _post(xp, o, y, ga, gs, p_prompt[i].reshape(m_p, -1), w)

        q, k_s, v_s, u, ga, gs = _in_proj(xs, w)
        o = _attn_sample(q, k_s, v_s, ck, cv, i, page_table, bias_rows, dec_seq)
        y, h_s = _ssm(u, state_ssm[i], mats[dec_seq], i, ssm_d[i], dec_batch, dec_seq, dec_seq)
        xs = _post(xs, o, y, ga, gs, p_sample[i].reshape(m_s, -1), w)

        shape4 = (dec_batch, dec_seq, N_HEADS, HEAD_DIM)
        for lst, val in zip(outs, (h_p, k_s.reshape(shape4), v_s.reshape(shape4), h_s)):
            lst.append(val)
    s_p, k_s, v_s, s_s = (jnp.stack(o) for o in outs)
    from_t = lambda t: jnp.transpose(t.reshape(depth, batch, N_HEADS, HEAD_DIM, seq), (0, 1, 4, 2, 3))
    return (xp.reshape(x_prompt.shape), xs.reshape(x_sample.shape), from_t(kt_all), from_t(vt_all), s_p,
            k_s, v_s, s_s)
```

```python
import functools

import jax
import jax.numpy as jnp
from jax import lax
from jax.experimental import pallas as pl
from jax.experimental.pallas import tpu as pltpu

F32 = jnp.float32
BF16 = jnp.bfloat16

N_HEADS = 8
HEAD_DIM = 64
ATT_WIDTH = N_HEADS * HEAD_DIM
GROUP_CH = 16
N_GROUPS = 32
SSM_STATE = 64
D_SSM = N_GROUPS * GROUP_CH
RMS_EPS = 1e-6
LOG2E = 1.4426950408889634
Q_SCALE = -(HEAD_DIM ** -0.5) * LOG2E
LANES = 128
MXU_TILE = 256
GROUPS_PER_BLOCK = LANES // GROUP_CH
VMEM_LIMIT = 56 * 1024 * 1024

PROMPT_CHUNK = 16
SSM_ROW_CHUNK = 64
ATT_TK = 256
ATT_TQ = 2 * ATT_TK
ROW_TILE = 256
FF_CHUNK = 1024


def _dot(a, b):
    return jnp.dot(a, b, preferred_element_type=F32)


def _dot_nt(a, b):
    return lax.dot_general(a, b, (((1,), (1,)), ((), ())), preferred_element_type=F32)


def _split(x):
    hi = x.astype(BF16)
    lo = (x - hi.astype(F32)).astype(BF16)
    return hi, lo


def _dot3(a, b):
    ah, al = _split(a)
    bh, bl = _split(b)
    return _dot(ah, bh) + _dot(al, bh) + _dot(ah, bl)


def _params(sem):
    return pltpu.CompilerParams(dimension_semantics=sem, vmem_limit_bytes=VMEM_LIMIT)


def _resident(shape):
    nd = len(shape)
    return pl.BlockSpec(shape, lambda *_: (0,) * nd, pipeline_mode=pl.Buffered(1))


def _in_proj_kernel(x_ref, gmix_ref, w_ref, bd_ref, gq_ref, gk_ref, *rest, kv_transposed, n_prev):
    x = x_ref[...]
    ms = jnp.mean(x * x, axis=-1, keepdims=True)
    h = (x * lax.rsqrt(ms + RMS_EPS) * gmix_ref[...]).astype(BF16)
    a = ATT_WIDTH

    def head_norm(p, g_ref):
        msh = _dot((p * p).astype(BF16), bd_ref[...])
        return p * lax.rsqrt(msh + RMS_EPS) * g_ref[...]

    if kv_transposed:
        wkvt_ref = rest[0]
        q_ref, k_ref, v_ref, u_ref, ga_ref, gs_ref = rest[1 + n_prev:]
        kvt = jnp.concatenate([_dot_nt(wkvt_ref[r:r + MXU_TILE, :], h) for r in range(0, 2 * a, MXU_TILE)], axis=0)
        kt = kvt[:a]
        mst = _dot(bd_ref[...], (kt * kt).astype(BF16))
        k_ref[...] = kt * lax.rsqrt(mst + RMS_EPS) * gk_ref[...]
        v_ref[...] = kvt[a:]
    else:
        q_ref, k_ref, v_ref, u_ref, ga_ref, gs_ref = rest
        k_ref[...] = head_norm(_dot(h, w_ref[:, a:2 * a]), gk_ref)
        v_ref[...] = _dot(h, w_ref[:, 2 * a:3 * a])
    q_ref[...] = (head_norm(_dot(h, w_ref[:, 0:a]), gq_ref) * Q_SCALE).astype(BF16)
    u_ref[...] = _dot(h, w_ref[:, 3 * a:3 * a + D_SSM])
    o = 3 * a + D_SSM
    d = ga_ref.shape[-1]
    ga_ref[...] = jax.nn.sigmoid(_dot(h, w_ref[:, o:o + d])).astype(BF16)
    gs_ref[...] = jax.nn.sigmoid(_dot(h, w_ref[:, o + d:o + 2 * d])).astype(BF16)


def _in_proj(x, w, kv_slot=None):
    m, d = x.shape
    tm = min(ROW_TILE, m)
    row = lambda width: pl.BlockSpec((tm, width), lambda i: (i, 0))
    tok = lambda width, t: (row(width), jax.ShapeDtypeStruct((m, width), t))
    ins = [x, w["g_mix"].reshape(1, d), w["w_in"], w["bd"], w["g_q"]]
    in_specs = [row(d), _resident((1, d)), _resident(w["w_in"].shape), _resident(w["bd"].shape),
                _resident((1, ATT_WIDTH))]
    aliases = {}
    n_prev = 0
    if kv_slot is None:
        ins.append(w["g_k"])
        in_specs.append(_resident((1, ATT_WIDTH)))
        kv = [tok(ATT_WIDTH, F32)] * 2
    else:
        layer, depth, batch, seq, prev_kt, prev_vt = kv_slot
        per_seq = seq // tm
        ins += [w["g_k"].reshape(ATT_WIDTH, 1), w["w_kvt"]]
        in_specs += [_resident((ATT_WIDTH, 1)), _resident(w["w_kvt"].shape)]
        if prev_kt is not None:
            n_prev = 2
            aliases = {len(ins): 1, len(ins) + 1: 2}
            ins += [prev_kt, prev_vt]
            in_specs += [pl.BlockSpec(memory_space=pl.ANY)] * 2
        slot = pl.BlockSpec((None, None, ATT_WIDTH, tm), lambda i: (layer, i // per_seq, 0, i % per_seq))
        kv = [(slot, jax.ShapeDtypeStruct((depth, batch, ATT_WIDTH, seq), F32))] * 2
    outs = [tok(ATT_WIDTH, BF16)] + kv + [tok(D_SSM, F32), tok(d, BF16), tok(d, BF16)]
    return pl.pallas_call(
        functools.partial(_in_proj_kernel, kv_transposed=kv_slot is not None, n_prev=n_prev),
        grid=(m // tm,),
        in_specs=in_specs,
        out_specs=[o[0] for o in outs],
        out_shape=[o[1] for o in outs],
        input_output_aliases=aliases,
        compiler_params=_params(("parallel",)),
        name="in_proj",
    )(*ins)


def _log_stay(nz, mask):
    m = jnp.minimum(nz, 0.0)
    ls = m - jnp.log(1.0 + jnp.exp2(m + (m - nz))) * LOG2E
    return ls if mask is None else jnp.where(mask, ls, 0.0)


def _prefix_operand(ls, copies):
    return ls.astype(BF16) if copies == 1 else jnp.concatenate(_split(ls), axis=1)


def _sb_weights(ls, nz, later, carry, mask):
    after = later + carry
    w = jnp.exp2((ls - nz) + after)
    if mask is not None:
        w = jnp.where(mask, w, 0.0)
    return w, after[:, 0:1] + ls[:, 0:1]


def _sb_block(nz, mask, tri, carry):
    ls = _log_stay(nz, mask)
    later = _dot(_prefix_operand(ls, tri.shape[0] // tri.shape[1]), tri)
    return _sb_weights(ls, nz, later, carry, mask)


def _later_keys(n, copies):
    assert copies in (1, 2)
    r = lax.broadcasted_iota(jnp.int32, (copies * n, n), 0)
    c = lax.broadcasted_iota(jnp.int32, (copies * n, n), 1)
    return jnp.where(jnp.where(r >= n, r - n, r) > c, 1.0, 0.0).astype(BF16)


def _attn_prompt_kernel(bias_ref, q_ref, kt_ref, vt_ref, o_ref, kb_s, vb_s, acc_ref, carry_ref,
                        *, tq, tk, seq):
    hp = pl.program_id(1)
    qi = pl.program_id(2)
    lane = lax.broadcasted_iota(jnp.int32, (1, LANES), 1)
    first_head = lane < HEAD_DIM

    @pl.when(qi == 0)
    def _stage_keys_values():
        ones = jnp.where(lax.broadcasted_iota(jnp.int32, (LANES, tk), 0) < 2, 1.0, 0.0)
        for i in range(seq // tk):
            kb_s[i] = jnp.concatenate([kt_ref[:, i * tk:(i + 1) * tk], ones], axis=0).astype(BF16)
            vb_s[i] = vt_ref[:, i * tk:(i + 1) * tk].astype(BF16)

    qs = q_ref[...]
    q2 = jnp.concatenate([jnp.where(first_head, qs, 0), jnp.where(first_head, 0, qs)], axis=0)

    def bias_lanes(h):
        b = jnp.full((tq, LANES), bias_ref[2 * hp + h] * (-LOG2E), F32)
        b_hi = b.astype(BF16).astype(F32)
        return jnp.where(lane == 0, b_hi, jnp.where(lane == 1, b - b_hi, 0.0)).astype(BF16)
    q2 = jnp.concatenate([q2, jnp.concatenate([bias_lanes(0), bias_lanes(1)], axis=0)], axis=1)
    tri = _later_keys(tk, 1)

    def block(kb, mask, carry, q_rows):
        n = q_rows.shape[0] // 2
        nz = _dot(q_rows, kb_s[kb])
        w, carry = _sb_block(nz, mask, tri, carry)
        pv = _dot_nt(w.astype(BF16), vb_s[kb])
        return jnp.where(first_head, pv[:n], pv[n:]), carry

    assert tq == 2 * tk
    top = 2 * qi + 1
    r = lax.broadcasted_iota(jnp.int32, (2 * tk, tk), 0)
    c = lax.broadcasted_iota(jnp.int32, (2 * tk, tk), 1)
    triangle = c < jnp.where(r < tk, r, r - tk)
    half = lambda x, lo: jnp.concatenate([x[lo:lo + tk], x[tq + lo:tq + lo + tk]], axis=0)
    q_early, q_late = half(q2, 0), half(q2, tk)
    zero = jnp.zeros((2 * tk, 1), F32)
    def diagonal(n_extra):
        late_top, c_late = block(top, triangle, zero, q_late)
        late_full, c_late = block(top - 1, None, c_late, q_late)
        early, c_early = block(top - 1, triangle, zero, q_early)
        carry = jnp.concatenate([c_early[:tk], c_late[:tk], c_early[tk:], c_late[tk:]], axis=0)
        total = jnp.concatenate([early, late_top + late_full], axis=0)
        for d in range(n_extra):
            contrib, carry = block(top - 2 - d, None, carry, q2)
            total = total + contrib
        acc_ref[...] = total
        carry_ref[...] = carry

    odd = jnp.bitwise_and(qi, 1)

    @pl.when(odd == 1)
    def _diagonal_and_pair():
        diagonal(2)

    @pl.when(odd == 0)
    def _diagonal_only():
        diagonal(0)

    def below(first, n):
        carry = carry_ref[...]
        total = None
        for d in range(n):
            contrib, carry = block(first - d, None, carry, q2)
            total = contrib if total is None else total + contrib
        acc_ref[...] += total
        carry_ref[...] = carry

    def four(i, _):
        below(top - 2 - 2 * odd - 4 * i, 4)
        return 0
    lax.fori_loop(0, lax.shift_right_logical(qi, 1), four, 0)
    o_ref[...] = acc_ref[...].astype(BF16)


def _attn_prompt(q, kt_all, vt_all, layer, bias, batch, seq):
    tq, tk = ATT_TQ, ATT_TK
    nq = seq // tq
    kern = functools.partial(_attn_prompt_kernel, tq=tq, tk=tk, seq=seq)
    head_pair = pl.BlockSpec((None, None, LANES, seq), lambda b, hp, i: (layer, b, hp, 0))
    return pl.pallas_call(
        kern,
        grid=(batch, N_HEADS // 2, nq),
        in_specs=[pl.BlockSpec(memory_space=pltpu.SMEM),
                  pl.BlockSpec((tq, LANES), lambda b, hp, i: (b * nq + i, hp)),
                  head_pair, head_pair],
        out_specs=pl.BlockSpec((tq, LANES), lambda b, hp, i: (b * nq + i, hp)),
        out_shape=jax.ShapeDtypeStruct(q.shape, BF16),
        scratch_shapes=[pltpu.VMEM((seq // tk, 2 * LANES, tk), BF16), pltpu.VMEM((seq // tk, LANES, tk), BF16)]
                       + [pltpu.VMEM((tq, LANES), F32), pltpu.VMEM((2 * tq, 1), F32)],
        compiler_params=_params(("parallel", "parallel", "arbitrary")),
        name="attn_prompt",
    )(bias, q, kt_all, vt_all)


def _attn_sample_kernel(pt_ref, bias_ref, hmask_ref, nmask_ref, q_ref, kn_ref, vn_ref, *rest,
                        n_pages, t_new, page):
    del pt_ref
    k_pages = rest[:n_pages]
    v_pages = rest[n_pages:2 * n_pages]
    o_ref = rest[2 * n_pages]
    rows = N_HEADS * t_new
    same_head = hmask_ref[...] > 0.5
    q = q_ref[...].astype(F32)
    q_bd = jnp.where(same_head, jnp.concatenate([q] * N_HEADS, axis=0), 0.0).astype(BF16)
    nbias = bias_ref[...] * (-LOG2E)

    pad = jnp.zeros((page - t_new, ATT_WIDTH), F32)
    k_new = jnp.concatenate([kn_ref[...], pad], axis=0).astype(BF16)
    v_new = jnp.concatenate([vn_ref[...], pad], axis=0).astype(BF16)
    new_mask = nmask_ref[...] > 0.5
    nz = _dot_nt(q_bd, k_new) + nbias
    w, carry = _sb_block(nz, new_mask, _later_keys(page, 2), jnp.zeros((rows, 1), F32))
    acc = _dot(w.astype(BF16), v_new)

    per_block = 2 if n_pages % 2 == 0 else 1
    bw = per_block * page
    n_blk = n_pages // per_block
    all_pages = lambda refs: jnp.concatenate([r[...] for r in refs], axis=1).astype(BF16)
    nz = _dot(q_bd, all_pages(k_pages)) + nbias
    ls = _log_stay(nz, None)
    blocks = [slice(jb * bw, (jb + 1) * bw) for jb in range(n_blk)]
    later = _dot(jnp.concatenate([_prefix_operand(ls[:, b], 2) for b in blocks], axis=0), _later_keys(bw, 2))
    laters = [later[jb * rows:(jb + 1) * rows] for jb in range(n_blk)]
    sums = [laters[jb][:, 0:1] + ls[:, blocks[jb]][:, 0:1] for jb in range(n_blk)]
    ws = [None] * n_blk
    for jb in reversed(range(n_blk)):
        b = blocks[jb]
        ws[jb] = _sb_weights(ls[:, b], nz[:, b], laters[jb], carry, None)[0].astype(BF16)
        carry = carry + sums[jb]
    acc = acc + _dot_nt(jnp.concatenate(ws, axis=1), all_pages(v_pages))
    acc = jnp.where(same_head, acc, 0.0)
    out = acc[0:t_new, :]
    for h in range(1, N_HEADS):
        out = out + acc[h * t_new:(h + 1) * t_new, :]
    o_ref[...] = out.astype(BF16)


def _attn_sample(q, k_new, v_new, cache_k, cache_v, layer, page_table, bias_rows, t_new):
    n_seq, n_pages = page_table.shape
    page = cache_k.shape[3]
    page_spec = lambda j: pl.BlockSpec((None, None, ATT_WIDTH, page),
                                       lambda b, pt, j=j: (layer, pt[b, j], 0, 0))
    tok = pl.BlockSpec((t_new, ATT_WIDTH), lambda b, pt: (b, 0))
    seq_blk = pl.BlockSpec((None, t_new, ATT_WIDTH), lambda b, pt: (b, 0, 0))
    rows = N_HEADS * t_new
    whole = lambda w: pl.BlockSpec((rows, w), lambda b, pt: (0, 0))
    row_head = jnp.arange(rows, dtype=jnp.int32)[:, None] // t_new
    row_query = jnp.arange(rows, dtype=jnp.int32)[:, None] % t_new
    head_mask = (row_head == jnp.arange(ATT_WIDTH, dtype=jnp.int32)[None, :] // HEAD_DIM).astype(F32)
    new_mask = (jnp.arange(page, dtype=jnp.int32)[None, :] < row_query).astype(F32)
    kern = functools.partial(_attn_sample_kernel, n_pages=n_pages, t_new=t_new, page=page)
    grid_spec = pltpu.PrefetchScalarGridSpec(
        num_scalar_prefetch=1,
        grid=(n_seq,),
        in_specs=[whole(1), whole(ATT_WIDTH), whole(page), seq_blk, tok, tok]
                 + [page_spec(j) for j in range(n_pages)] * 2,
        out_specs=seq_blk,
    )
    o = pl.pallas_call(
        kern,
        grid_spec=grid_spec,
        out_shape=jax.ShapeDtypeStruct((n_seq, t_new, ATT_WIDTH), BF16),
        compiler_params=_params(("arbitrary",)),
        name="attn_sample",
    )(page_table, bias_rows, head_mask, new_mask, q.reshape(n_seq, t_new, ATT_WIDTH), k_new, v_new,
      *([cache_k] * n_pages), *([cache_v] * n_pages))
    return o.reshape(q.shape)


def _ssm_kernel(u_ref, d_ref, m_ref, wre_ref, wim_ref, vre_ref, vim_ref, a_ref, h0re_ref, h0im_ref,
                y_ref, hre_ref, him_ref, ug_s, yg_s, zre_s, zim_s, inre_s, inim_s, *, L, n_steps, bp):
    n_rows = n_steps * bp
    p = SSM_STATE
    gpb = GROUPS_PER_BLOCK
    slot = lax.shift_right_logical(lax.broadcasted_iota(jnp.int32, (1, LANES), 1), GROUP_CH.bit_length() - 1)

    def transpose_slots(a):
        a = list(a)
        k = gpb // 2
        while k:
            upper = jnp.bitwise_and(slot, k) != 0
            for i in range(gpb):
                if not i & k:
                    lo, hi = a[i], a[i + k]
                    a[i] = jnp.where(upper, pltpu.roll(hi, k * GROUP_CH, 1), lo)
                    a[i + k] = jnp.where(upper, hi, pltpu.roll(lo, LANES - k * GROUP_CH, 1))
            k //= 2
        return a

    rc = min(n_rows, SSM_ROW_CHUNK)
    assert n_rows % rc == 0 and L % gpb == 0
    n_half = L // gpb
    for r0 in range(0, n_rows, rc):
        for h in range(n_half):
            steps = [u_ref[pl.ds(r0 * L + h * gpb + j, rc, stride=L), :] for j in range(gpb)]
            for g, u_gh in enumerate(transpose_slots(steps)):
                ug_s[g, r0:r0 + rc, h * LANES:(h + 1) * LANES] = u_gh
    for g in range(gpb):
        u_g = ug_s[g]
        zre_s[:, g * p:(g + 1) * p] = _dot3(u_g, wre_ref[g])
        zim_s[:, g * p:(g + 1) * p] = _dot3(u_g, wim_ref[g])

    ar = a_ref[0:1, :]
    ai = a_ref[1:2, :]

    def step(k, carry):
        hr, hi = carry
        rows = pl.ds(k * bp, bp)
        inre_s[rows, :] = hr
        inim_s[rows, :] = hi
        zr = zre_s[rows, :]
        zi = zim_s[rows, :]
        return ar * hr - ai * hi + zr, ar * hi + ai * hr + zi

    hr, hi = lax.fori_loop(0, n_steps, step, (h0re_ref[...], h0im_ref[...]))
    hre_ref[...] = hr
    him_ref[...] = hi

    bdot = lambda a, b: _dot(a.astype(BF16), b.astype(BF16))
    for g in range(gpb):
        yg_s[g] = (bdot(ug_s[g], m_ref[g]) + bdot(inre_s[:, g * p:(g + 1) * p], vre_ref[g])
                   + bdot(inim_s[:, g * p:(g + 1) * p], vim_ref[g]))
    for r0 in range(0, n_rows, rc):
        for h in range(n_half):
            groups = [yg_s[g, r0:r0 + rc, h * LANES:(h + 1) * LANES] for g in range(gpb)]
            for j, y_step in enumerate(transpose_slots(groups)):
                rows = pl.ds(r0 * L + h * gpb + j, rc, stride=L)
                y_ref[rows, :] = y_step + d_ref[...] * u_ref[rows, :]


def _ssm_mats(a_re, a_im, log_dt, b_re, b_im, c_re, c_im, lengths):
    hp = lax.Precision.HIGHEST
    L = max(lengths)
    dt = jnp.exp(log_dt)[:, None]
    mag = jnp.exp(a_re * dt)
    ab_re, ab_im = mag * jnp.cos(a_im * dt), mag * jnp.sin(a_im * dt)
    den = a_re * a_re + a_im * a_im
    nr, ni = ab_re - 1.0, ab_im
    f_re = (nr * a_re + ni * a_im) / den
    f_im = (ni * a_re - nr * a_im) / den
    bb_re = f_re[..., None] * b_re - f_im[..., None] * b_im
    bb_im = f_re[..., None] * b_im + f_im[..., None] * b_re
    pr, pi = [jnp.ones_like(ab_re)], [jnp.zeros_like(ab_re)]
    for _ in range(L):
        pr, pi = pr + [ab_re * pr[-1] - ab_im * pi[-1]], pi + [ab_re * pi[-1] + ab_im * pr[-1]]
    p_re, p_im = jnp.stack(pr), jnp.stack(pi)
    abr = p_re[:L, :, :, None] * bb_re - p_im[:L, :, :, None] * bb_im
    abi = p_re[:L, :, :, None] * bb_im + p_im[:L, :, :, None] * bb_re
    kern = (jnp.einsum('gcp,tgpd->tgcd', c_re, abr, precision=hp)
            - jnp.einsum('gcp,tgpd->tgcd', c_im, abi, precision=hp))
    lc = L * GROUP_CH
    resp = jnp.transpose(kern, (1, 3, 0, 2)).reshape(N_GROUPS, GROUP_CH, lc)
    m = jnp.stack([jnp.pad(resp[:, :, :lc - s * GROUP_CH], ((0, 0), (0, 0), (s * GROUP_CH, 0)))
                   for s in range(L)], axis=1).reshape(N_GROUPS, lc, lc)
    idx = jnp.arange(L)
    flip = (idx[:, None] + idx[None, :] == L - 1).astype(F32)
    w_re = jnp.einsum('sz,zgpd->gsdp', flip, abr, precision=hp).reshape(N_GROUPS, lc, SSM_STATE)
    w_im = jnp.einsum('sz,zgpd->gsdp', flip, abi, precision=hp).reshape(N_GROUPS, lc, SSM_STATE)
    cr = jnp.transpose(c_re, (0, 2, 1))[:, :, None, :]
    ci = jnp.transpose(c_im, (0, 2, 1))[:, :, None, :]
    qr = jnp.transpose(p_re[1:], (1, 2, 0))[..., None]
    qi = jnp.transpose(p_im[1:], (1, 2, 0))[..., None]
    v_re = (cr * qr - ci * qi).reshape(N_GROUPS, SSM_STATE, lc)
    v_im = (-cr * qi - ci * qr).reshape(N_GROUPS, SSM_STATE, lc)

    def for_length(n):
        nc, skip = n * GROUP_CH, (L - n) * GROUP_CH
        a_n = jnp.stack([p_re[n], p_im[n]], axis=1)
        return (m[:, :nc, :nc], w_re[:, skip:], w_im[:, skip:], v_re[:, :, :nc], v_im[:, :, :nc], a_n)

    return {n: for_length(n) for n in lengths}


def _ssm(u, h0, mats, layer, d_skip, batch, seq, L):
    m, w_re, w_im, v_re, v_im, a_l = mats
    depth = m.shape[0]
    gpb = GROUPS_PER_BLOCK
    n_cb = N_GROUPS // gpb
    lc = L * GROUP_CH
    state_w = gpb * SSM_STATE
    if seq == L:
        n_bblk, bp, n_steps = 1, batch, 1
    else:
        n_bblk, bp, n_steps = batch, 1, seq // L
    tok = bp * n_steps * L
    n_rows = bp * n_steps

    def to_blocks(x):
        x = x.reshape(n_bblk, bp, n_cb, state_w)
        return jnp.transpose(x, (0, 2, 1, 3))

    if h0 is None:
        h0re = h0im = jnp.zeros((n_bblk, n_cb, bp, state_w), F32)
    else:
        h0re, h0im = to_blocks(h0[..., 0]), to_blocks(h0[..., 1])
    a_blk = jnp.transpose(a_l.reshape(depth, n_cb, gpb, 2, SSM_STATE), (0, 1, 3, 2, 4))
    a_blk = a_blk.reshape(depth, n_cb, 2, state_w)

    grp = lambda a, b: pl.BlockSpec((None, gpb, a, b), lambda i, cb: (layer, cb, 0, 0))
    act = pl.BlockSpec((tok, LANES), lambda i, cb: (i, cb))
    state = pl.BlockSpec((None, None, bp, state_w), lambda i, cb: (i, cb, 0, 0))
    state_shape = jax.ShapeDtypeStruct((n_bblk, n_cb, bp, state_w), F32)
    kern = functools.partial(_ssm_kernel, L=L, n_steps=n_steps, bp=bp)
    y, hre, him = pl.pallas_call(
        kern,
        grid=(n_bblk, n_cb),
        in_specs=[act, pl.BlockSpec((1, LANES), lambda i, cb: (0, cb)),
                  grp(lc, lc), grp(lc, SSM_STATE), grp(lc, SSM_STATE),
                  grp(SSM_STATE, lc), grp(SSM_STATE, lc),
                  pl.BlockSpec((None, None, 2, state_w), lambda i, cb: (layer, cb, 0, 0)), state, state],
        out_specs=[act, state, state],
        out_shape=[jax.ShapeDtypeStruct(u.shape, F32), state_shape, state_shape],
        scratch_shapes=[pltpu.VMEM((gpb, n_rows, lc), F32)] * 2 + [pltpu.VMEM((n_rows, state_w), F32)] * 4,
        compiler_params=_params(("parallel", "parallel")),
        name="ssm",
    )(u, d_skip.reshape(1, D_SSM), m, w_re, w_im, v_re, v_im, a_blk, h0re, h0im)

    def from_blocks(x):
        return jnp.transpose(x, (0, 2, 1, 3)).reshape(batch, N_GROUPS, SSM_STATE)

    return y, jnp.stack([from_blocks(hre), from_blocks(him)], axis=-1)


def _post_kernel(x_ref, o_ref, y_ref, ga_ref, gs_ref, p_ref,
                 wao_ref, wgv_ref, wgg_ref, wout_ref, gmlp_ref, wup_ref, wdown_ref,
                 gple_ref, wple_ref, wpg_ref, out_ref, *, ff_chunk):
    def rms(x, g_ref):
        ms = jnp.mean(x * x, axis=-1, keepdims=True)
        return (x * lax.rsqrt(ms + RMS_EPS) * g_ref[...]).astype(BF16)

    y_act = jax.nn.gelu(y_ref[...], approximate=True).astype(BF16)
    attn_out = _dot(o_ref[...], wao_ref[...])
    ssm_out = _dot(y_act, wgv_ref[...]) * jax.nn.sigmoid(_dot(y_act, wgg_ref[...]))
    merged = ga_ref[...].astype(F32) * attn_out + gs_ref[...].astype(F32) * ssm_out
    x = x_ref[...] + _dot(merged.astype(BF16), wout_ref[...])
    h2 = rms(x, gmlp_ref)
    d_ff = wup_ref.shape[1]
    mlp = None
    for c0 in range(0, d_ff, ff_chunk):
        hid = jnp.square(jnp.maximum(_dot(h2, wup_ref[:, c0:c0 + ff_chunk]), 0.0)).astype(BF16)
        part = _dot(hid, wdown_ref[c0:c0 + ff_chunk, :])
        mlp = part if mlp is None else mlp + part
    x = x + mlp
    h3 = rms(x, gple_ref)
    out_ref[...] = x + _dot(p_ref[...].astype(BF16), wple_ref[...]) * jax.nn.sigmoid(_dot(h3, wpg_ref[...]))


def _post(x, o, y, ga, gs, p, w):
    m, d = x.shape
    tm = min(ROW_TILE, m)
    row = lambda a: pl.BlockSpec((tm, a.shape[1]), lambda i: (i, 0))
    vec = lambda a: a.reshape(1, -1)
    acts = (x, o, y, ga, gs, p)
    consts = (w["w_attn_out"], w["w_glu_val"], w["w_glu_gate"], w["w_out"], vec(w["g_mlp"]),
              w["w_up"], w["w_down"], vec(w["g_ple"]), w["w_ple"], w["w_ple_gate"])
    return pl.pallas_call(
        functools.partial(_post_kernel, ff_chunk=FF_CHUNK),
        grid=(m // tm,),
        in_specs=[row(a) for a in acts] + [_resident(c.shape) for c in consts],
        out_specs=pl.BlockSpec((tm, d), lambda i: (i, 0)),
        out_shape=jax.ShapeDtypeStruct((m, d), F32),
        compiler_params=_params(("parallel",)),
        name="post",
    )(*acts, *consts)


def kernel(x_prompt, x_sample, p_prompt, p_sample, cache_k, cache_v, state_ssm, page_table, g_mix, w_in, g_q, g_k, sb_bias, w_attn_out, ssm_a_re, ssm_a_im, ssm_log_dt, ssm_b_re, ssm_b_im, ssm_c_re, ssm_c_im, ssm_d, w_glu_val, w_glu_gate, w_out, g_mlp, w_up, w_down, g_ple, w_ple, w_ple_gate):
    depth = w_in.shape[0]
    batch, seq, d_model = x_prompt.shape
    dec_batch, dec_seq, _ = x_sample.shape
    m_p, m_s = batch * seq, dec_batch * dec_seq
    to_pages = lambda c: jnp.transpose(c, (0, 1, 3, 4, 2)).reshape(c.shape[:2] + (ATT_WIDTH, c.shape[2]))
    ck, cv = to_pages(cache_k), to_pages(cache_v)
    hd = jnp.arange(ATT_WIDTH, dtype=jnp.int32) // HEAD_DIM
    bd = jnp.where(hd[:, None] == hd[None, :], 1.0 / HEAD_DIM, 0.0).astype(BF16)

    mats = jax.vmap(lambda *p: _ssm_mats(*p, sorted({PROMPT_CHUNK, dec_seq})))(
        ssm_a_re, ssm_a_im, ssm_log_dt, ssm_b_re, ssm_b_im, ssm_c_re, ssm_c_im)

    xp, xs = x_prompt.reshape(m_p, d_model), x_sample.reshape(m_s, d_model)
    kt_all = vt_all = None
    outs = [[] for _ in range(4)]
    for i in range(depth):
        w_in_bf = w_in[i].astype(BF16)
        w = {
            "g_mix": g_mix[i], "w_in": w_in_bf, "bd": bd,
            "w_kvt": w_in_bf[:, ATT_WIDTH:3 * ATT_WIDTH].T,
            "g_q": jnp.tile(g_q[i], N_HEADS).reshape(1, ATT_WIDTH),
            "g_k": jnp.tile(g_k[i], N_HEADS).reshape(1, ATT_WIDTH),
            "w_attn_out": w_attn_out[i].astype(BF16), "w_glu_val": w_glu_val[i].astype(BF16),
            "w_glu_gate": w_glu_gate[i].astype(BF16), "w_out": w_out[i].astype(BF16),
            "g_mlp": g_mlp[i], "w_up": w_up[i].astype(BF16), "w_down": w_down[i].astype(BF16),
            "g_ple": g_ple[i], "w_ple": w_ple[i].astype(BF16), "w_ple_gate": w_ple_gate[i].astype(BF16),
        }
        bias = sb_bias[i].astype(F32)
        bias_rows = jnp.repeat(bias, dec_seq).reshape(N_HEADS * dec_seq, 1)

        q, kt_all, vt_all, u, ga, gs = _in_proj(xp, w, (i, depth, batch, seq, kt_all, vt_all))
        o = _attn_prompt(q, kt_all, vt_all, i, bias, batch, seq)
        y, h_p = _ssm(u, None, mats[PROMPT_CHUNK], i, ssm_d[i], batch, seq, PROMPT_CHUNK)
        xp = _post(xp, o, y, ga, gs, p_prompt[i].reshape(m_p, -1), w)

        q, k_s, v_s, u, ga, gs = _in_proj(xs, w)
        o = _attn_sample(q, k_s, v_s, ck, cv, i, page_table, bias_rows, dec_seq)
        y, h_s = _ssm(u, state_ssm[i], mats[dec_seq], i, ssm_d[i], dec_batch, dec_seq, dec_seq)
        xs = _post(xs, o, y, ga, gs, p_sample[i].reshape(m_s, -1), w)

        shape4 = (dec_batch, dec_seq, N_HEADS, HEAD_DIM)
        for lst, val in zip(outs, (h_p, k_s.reshape(shape4), v_s.reshape(shape4), h_s)):
            lst.append(val)
    s_p, k_s, v_s, s_s = (jnp.stack(o) for o in outs)
    from_t = lambda t: jnp.transpose(t.reshape(depth, batch, N_HEADS, HEAD_DIM, seq), (0, 1, 4, 2, 3))
    return (xp.reshape(x_prompt.shape), xs.reshape(x_sample.shape), from_t(kt_all), from_t(vt_all), s_p,
            k_s, v_s, s_s)
```
